```python
import jax, jax.numpy as jnp
from jax import lax
import numpy as np

D_MODEL = 1024
BATCH = 1
SEQ = 16384
DEPTH = 1
DEC_BATCH = 8
DEC_SEQ = 64
PAST_LEN = 4096

CHUNK = 64
QBLOCK = 128
HEAD_DIM = 64
N_HEADS = (D_MODEL // 2) // HEAD_DIM
N_KV_HEADS = 2
IDX_HEADS = 8
IDX_DIM = 32
TOPK_MAX = 256
MEM_TOKENS = 256
MEM_HEADS = 4
MEM_WIDTH = MEM_HEADS * HEAD_DIM
POOL_WIDTH = D_MODEL // 4
POOL_GROUPS = 4
POOL_WINDOWS = (2, 4, 8, 16)
POOL_HIST = 15
ATTN_WIDTH = N_HEADS * HEAD_DIM
MIX_WIDTH = POOL_WIDTH + ATTN_WIDTH + MEM_WIDTH
D_FF = 2816
CONV_W = 3
ROPE_THETA = 500000.0
ROT_DIM = HEAD_DIM // 4
IDX_ROT_DIM = IDX_DIM // 4
EPS = 1e-6
SPLITS = (POOL_WIDTH, N_HEADS * HEAD_DIM, N_KV_HEADS * HEAD_DIM, N_KV_HEADS * HEAD_DIM,
          IDX_HEADS * IDX_DIM, IDX_DIM, IDX_HEADS, MEM_WIDTH)
IN_WIDTH = sum(SPLITS)

kernel_name = "hybrid_pool_dsa_memory_streaming_step"


def rms_norm(x, g):
    xf = x.astype(jnp.float32)
    y = xf * lax.rsqrt(jnp.mean(xf * xf, axis=-1, keepdims=True) + EPS)
    return (y * g.astype(jnp.float32)).astype(x.dtype)


def partial_rope(x, pos, rot_dim):
    half = rot_dim // 2
    inv = ROPE_THETA ** (-jnp.arange(half, dtype=jnp.float32) / half)
    ang = pos.astype(jnp.float32)[:, None] * inv[None, :]
    cos = jnp.cos(ang)[:, None, :]
    sin = jnp.sin(ang)[:, None, :]
    xr = x[..., :rot_dim].astype(jnp.float32)
    x1, x2 = xr[..., :half], xr[..., half:]
    rot = jnp.concatenate([x1 * cos - x2 * sin, x2 * cos + x1 * sin], axis=-1)
    return jnp.concatenate([rot.astype(x.dtype), x[..., rot_dim:]], axis=-1)


def project(x, pos, lw):
    B, T, _ = x.shape
    n = rms_norm(x, lw['norm1'])
    proj = n @ lw['w_in']
    cuts = [int(c) for c in np.cumsum(SPLITS)[:-1]]
    u, q, k, v, qi, ki, wi, qm = jnp.split(proj, cuts, axis=-1)
    q = partial_rope(rms_norm(q.reshape(B, T, N_HEADS, HEAD_DIM), lw['q_norm']), pos, ROT_DIM)
    k = partial_rope(rms_norm(k.reshape(B, T, N_KV_HEADS, HEAD_DIM), lw['k_norm']), pos, ROT_DIM)
    v = v.reshape(B, T, N_KV_HEADS, HEAD_DIM)
    qi = partial_rope(qi.reshape(B, T, IDX_HEADS, IDX_DIM), pos, IDX_ROT_DIM) * (IDX_DIM ** -0.5)
    ki = partial_rope(ki.reshape(B, T, 1, IDX_DIM), pos, IDX_ROT_DIM)[:, :, 0]
    wi = wi * (IDX_HEADS ** -0.5)
    qm = rms_norm(qm.reshape(B, T, MEM_HEADS, HEAD_DIM), lw['mem_q_norm'])
    return u, q, k, v, qi, ki, wi, qm


def pool_mixer(u, hist, pos0, lw):
    B, T, W = u.shape
    ext = jnp.concatenate([hist, u], axis=1)
    extf = ext.astype(jnp.float32)
    cs = jnp.concatenate([jnp.zeros((B, 1, W), jnp.float32), jnp.cumsum(extf, axis=1)], axis=1)
    pos = pos0 + jnp.arange(T)
    gw = W // POOL_GROUPS
    means = []
    for g, w in enumerate(POOL_WINDOWS):
        c0, c1 = g * gw, (g + 1) * gw
        s = cs[:, POOL_HIST + 1:POOL_HIST + 1 + T, c0:c1] - cs[:, POOL_HIST + 1 - w:POOL_HIST + 1 - w + T, c0:c1]
        cnt = jnp.minimum(w, pos + 1).astype(jnp.float32)[None, :, None]
        means.append(s / cnt)
    z = (jnp.concatenate(means, axis=-1) - u.astype(jnp.float32)).reshape(B, T, POOL_GROUPS, gw)
    y = jnp.einsum('btgc,gcd->btgd', z, lw['pool_w'].astype(jnp.float32)).reshape(B, T, W)
    y = y * lw['pool_scale'].astype(jnp.float32)
    return y.astype(u.dtype), ext[:, -POOL_HIST:]


def dsa_block(q, qi, wi, limit, k, v, ki, top_k):
    B, T = q.shape[0], q.shape[1]
    L = k.shape[1]
    s = jnp.einsum('bthd,bsd->bths', qi.astype(jnp.float32), ki.astype(jnp.float32))
    score = jnp.einsum('bth,bths->bts', wi.astype(jnp.float32), jax.nn.relu(s))
    visible = jnp.arange(L)[None, :] < limit[:, None]
    score = jnp.where(visible[None], score, -jnp.inf)
    top_val, top_idx = lax.top_k(score, top_k)
    valid = jnp.isfinite(top_val)
    take = jax.vmap(lambda rows, idx: rows[idx])
    k_sel = take(k, top_idx).astype(jnp.float32)
    v_sel = take(v, top_idx).astype(jnp.float32)
    qg = q.reshape(B, T, N_KV_HEADS, N_HEADS // N_KV_HEADS, HEAD_DIM).astype(jnp.float32)
    logits = jnp.einsum('btjgd,btnjd->btjgn', qg, k_sel) * (HEAD_DIM ** -0.5)
    logits = jnp.where(valid[:, :, None, None, :], logits, -jnp.inf)
    p = jax.nn.softmax(logits, axis=-1)
    o = jnp.einsum('btjgn,btnjd->btjgd', p, v_sel)
    return o.reshape(B, T, N_HEADS * HEAD_DIM).astype(q.dtype)


def dsa_prompt(q, qi, wi, k, v, ki, top_k):
    B, S = q.shape[0], q.shape[1]
    nb = S // QBLOCK
    def blocks(a):
        return jnp.swapaxes(a.reshape((B, nb, QBLOCK) + a.shape[2:]), 0, 1)
    limit = ((jnp.arange(S) // CHUNK + 1) * CHUNK).reshape(nb, QBLOCK)
    out = lax.map(lambda a: dsa_block(a[0], a[1], a[2], a[3], k, v, ki, top_k),
                  (blocks(q), blocks(qi), blocks(wi), limit))
    return jnp.swapaxes(out, 0, 1).reshape(B, S, N_HEADS * HEAD_DIM)


def memory_kv(mem, lw):
    B, M, _ = mem.shape
    m = rms_norm(mem, lw['mem_norm'])
    mk = rms_norm((m @ lw['w_mem_k']).reshape(B, M, MEM_HEADS, HEAD_DIM), lw['mem_k_norm'])
    mv = (m @ lw['w_mem_v']).reshape(B, M, MEM_HEADS, HEAD_DIM)
    return mk, mv


def mem_attend(qm, mk, mv):
    B, T = qm.shape[0], qm.shape[1]
    logits = jnp.einsum('bthd,bmhd->bhtm', qm.astype(jnp.float32), mk.astype(jnp.float32)) * (HEAD_DIM ** -0.5)
    p = jax.nn.softmax(logits, axis=-1)
    o = jnp.einsum('bhtm,bmhd->bthd', p, mv.astype(jnp.float32))
    return o.reshape(B, T, MEM_WIDTH).astype(qm.dtype)


def out_and_ffn(x, a_pool, a_attn, a_mem, conv_hist, lw):
    h = x + jnp.concatenate([a_pool, a_attn, a_mem], axis=-1) @ lw['w_out']
    n = rms_norm(h, lw['norm2'])
    up = n @ lw['w_up']
    T = up.shape[1]
    ext = jnp.concatenate([conv_hist, up], axis=1)
    c = lw['conv_b']
    for j in range(CONV_W):
        c = c + ext[:, j:j + T] * lw['conv_w'][j]
    g, val = jnp.split(c, 2, axis=-1)
    y = h + (jax.nn.silu(g) * val) @ lw['w_down']
    return y, ext[:, -(CONV_W - 1):]


def prompt_layer(x, mem, lw):
    B, S, _ = x.shape
    pos = jnp.arange(S)
    u, q, k, v, qi, ki, wi, qm = project(x, pos, lw)
    a_pool, pool_state = pool_mixer(u, jnp.zeros((B, POOL_HIST, POOL_WIDTH), x.dtype), 0, lw)
    a_attn = dsa_prompt(q, qi, wi, k, v, ki, min(TOPK_MAX, S // 4))
    mk, mv = memory_kv(mem, lw)
    a_mem = mem_attend(qm, mk, mv)
    y, conv_state = out_and_ffn(x, a_pool, a_attn, a_mem,
                                jnp.zeros((B, CONV_W - 1, 2 * D_FF), x.dtype), lw)
    return y, (k, v, ki, mk, mv, pool_state, conv_state)


def sample_layer(x, ck, cv, cki, cmk, cmv, cpool, cconv, lw):
    B, T, _ = x.shape
    P = ck.shape[1]
    pos = P + jnp.arange(T)
    u, q, k, v, qi, ki, wi, qm = project(x, pos, lw)
    a_pool, pool_state = pool_mixer(u, cpool, P, lw)
    L = P + T
    k_all = jnp.concatenate([ck, k], axis=1)
    v_all = jnp.concatenate([cv, v], axis=1)
    ki_all = jnp.concatenate([cki, ki], axis=1)
    a_attn = dsa_block(q, qi, wi, jnp.full((T,), L, jnp.int32), k_all, v_all, ki_all, min(TOPK_MAX, L // 4))
    a_mem = mem_attend(qm, cmk, cmv)
    y, conv_state = out_and_ffn(x, a_pool, a_attn, a_mem, cconv, lw)
    return y, (k, v, ki, pool_state, conv_state)


def setup_inputs(seed: int = 0) -> dict:
    key = jax.random.key(seed)
    ks = jax.random.split(key, 32)
    f32 = jnp.float32
    def nrm(k, shape, scale=1.0):
        return jax.random.normal(k, shape, f32) * scale
    def gain(k, shape):
        return 1.0 + 0.02 * jax.random.normal(k, shape, f32)
    gw = POOL_WIDTH // POOL_GROUPS
    return {
        "x_prompt": nrm(ks[0], (BATCH, SEQ, D_MODEL)),
        "x_sample": nrm(ks[1], (DEC_BATCH, DEC_SEQ, D_MODEL)),
        "mem_prompt": nrm(ks[2], (BATCH, MEM_TOKENS, D_MODEL)),
        "cache_k": nrm(ks[3], (DEPTH, DEC_BATCH, PAST_LEN, N_KV_HEADS, HEAD_DIM)),
        "cache_v": nrm(ks[4], (DEPTH, DEC_BATCH, PAST_LEN, N_KV_HEADS, HEAD_DIM)),
        "cache_kidx": nrm(ks[5], (DEPTH, DEC_BATCH, PAST_LEN, IDX_DIM)),
        "cache_mem_k": nrm(ks[6], (DEPTH, DEC_BATCH, MEM_TOKENS, MEM_HEADS, HEAD_DIM)),
        "cache_mem_v": nrm(ks[7], (DEPTH, DEC_BATCH, MEM_TOKENS, MEM_HEADS, HEAD_DIM)),
        "state_pool": nrm(ks[8], (DEPTH, DEC_BATCH, POOL_HIST, POOL_WIDTH)),
        "state_ffn_conv": nrm(ks[9], (DEPTH, DEC_BATCH, CONV_W - 1, 2 * D_FF)),
        "norm1": gain(ks[10], (DEPTH, D_MODEL)),
        "w_in": nrm(ks[11], (DEPTH, D_MODEL, IN_WIDTH), D_MODEL ** -0.5),
        "q_norm": gain(ks[12], (DEPTH, HEAD_DIM)),
        "k_norm": gain(ks[13], (DEPTH, HEAD_DIM)),
        "pool_w": nrm(ks[14], (DEPTH, POOL_GROUPS, gw, gw), gw ** -0.5),
        "pool_scale": 1.0 + 0.1 * jax.random.normal(ks[15], (DEPTH, POOL_WIDTH), f32),
        "mem_norm": gain(ks[16], (DEPTH, D_MODEL)),
        "w_mem_k": nrm(ks[17], (DEPTH, D_MODEL, MEM_WIDTH), D_MODEL ** -0.5),
        "w_mem_v": nrm(ks[18], (DEPTH, D_MODEL, MEM_WIDTH), D_MODEL ** -0.5),
        "mem_q_norm": gain(ks[19], (DEPTH, HEAD_DIM)),
        "mem_k_norm": gain(ks[20], (DEPTH, HEAD_DIM)),
        "w_out": nrm(ks[21], (DEPTH, MIX_WIDTH, D_MODEL), MIX_WIDTH ** -0.5),
        "norm2": gain(ks[22], (DEPTH, D_MODEL)),
        "w_up": nrm(ks[23], (DEPTH, D_MODEL, 2 * D_FF), D_MODEL ** -0.5),
        "conv_w": nrm(ks[24], (DEPTH, CONV_W, 2 * D_FF), CONV_W ** -0.5),
        "conv_b": nrm(ks[25], (DEPTH, 2 * D_FF), 0.01),
        "w_down": nrm(ks[26], (DEPTH, D_FF, D_MODEL), D_FF ** -0.5),
    }


def reference(x_prompt, x_sample, mem_prompt, cache_k, cache_v, cache_kidx, cache_mem_k, cache_mem_v,
              state_pool, state_ffn_conv, norm1, w_in, q_norm, k_norm, pool_w, pool_scale, mem_norm,
              w_mem_k, w_mem_v, mem_q_norm, mem_k_norm, w_out, norm2, w_up, conv_w, conv_b, w_down):
    xp = x_prompt
    xs = x_sample
    p_states = []
    s_states = []
    for l in range(DEPTH):
        lw = {'norm1': norm1[l], 'w_in': w_in[l], 'q_norm': q_norm[l], 'k_norm': k_norm[l],
              'pool_w': pool_w[l], 'pool_scale': pool_scale[l], 'mem_norm': mem_norm[l],
              'w_mem_k': w_mem_k[l], 'w_mem_v': w_mem_v[l], 'mem_q_norm': mem_q_norm[l],
              'mem_k_norm': mem_k_norm[l], 'w_out': w_out[l], 'norm2': norm2[l], 'w_up': w_up[l],
              'conv_w': conv_w[l], 'conv_b': conv_b[l], 'w_down': w_down[l]}
        xp, st_p = prompt_layer(xp, mem_prompt, lw)
        p_states.append(st_p)
        xs, st_s = sample_layer(xs, cache_k[l], cache_v[l], cache_kidx[l], cache_mem_k[l], cache_mem_v[l],
                                state_pool[l], state_ffn_conv[l], lw)
        s_states.append(st_s)
    k_p, v_p, kidx_p, memk_p, memv_p, pool_p, conv_p = [jnp.stack(z) for z in zip(*p_states)]
    k_s, v_s, kidx_s, pool_s, conv_s = [jnp.stack(z) for z in zip(*s_states)]
    return (xp, xs, k_p, v_p, kidx_p, memk_p, memv_p, pool_p, conv_p, k_s, v_s, kidx_s, pool_s, conv_s)
```

```python
import functools
import math

import numpy as np
import jax
import jax.numpy as jnp
from jax import lax
from jax.experimental import pallas as pl
from jax.experimental.pallas import tpu as pltpu

F32, BF16, I32 = jnp.float32, jnp.bfloat16, jnp.int32

D_MODEL = 1024
CHUNK = 64
HEAD_DIM = 64
N_HEADS = 8
N_KV_HEADS = 2
IDX_HEADS = 8
IDX_DIM = 32
TOPK_MAX = 256
MEM_TOKENS = 256
MEM_HEADS = 4
MEM_WIDTH = MEM_HEADS * HEAD_DIM
POOL_WIDTH = 256
POOL_GROUPS = 4
POOL_WINDOWS = (2, 4, 8, 16)
POOL_HIST = 15
D_FF = 2816
CONV_W = 3
ROPE_THETA = 500000.0
ROT_DIM = HEAD_DIM // 4
IDX_ROT_DIM = IDX_DIM // 4
EPS = 1e-6
SPLITS = (POOL_WIDTH, N_HEADS * HEAD_DIM, N_KV_HEADS * HEAD_DIM, N_KV_HEADS * HEAD_DIM,
          IDX_HEADS * IDX_DIM, IDX_DIM, IDX_HEADS, MEM_WIDTH)

LANES = 128
SUBLANES = 8
VMEM_LIMIT = 56 * 1024 * 1024

COL_U = 0
COL_Q = 256
COL_K = 768
COL_V = 896
COL_QI = 1024
COL_KI = 1280
COL_WI = 1536
COL_QM = 1664
PROJ_W = 1920

QBLK = 128
KTILE = 512
LOG2E = 1.4426950408889634
Q_SCALE = HEAD_DIM ** -0.5 * LOG2E
FFN_CHUNK = 256
KEY_NEG_INF = -2139095041
MAX_SEARCH_STEPS = 40


def _dot(a, b):
    return jnp.dot(a, b, preferred_element_type=F32)


def _dot_nt(a, b):
    return lax.dot_general(a, b, (((1,), (1,)), ((), ())), preferred_element_type=F32)


def _full(shape):
    n = len(shape)
    return pl.BlockSpec(shape, lambda *_: (0,) * n)


def _resident(shape):
    n = len(shape)
    return pl.BlockSpec(shape, lambda *_: (0,) * n, pipeline_mode=pl.Buffered(1))


def _params(sem):
    return pltpu.CompilerParams(dimension_semantics=sem, vmem_limit_bytes=VMEM_LIMIT)


def _rms(x, g):
    ms = jnp.mean(x * x, axis=-1, keepdims=True)
    return x * lax.rsqrt(ms + EPS) * g


def _head_mean_sq(x, bmat):
    sq = x * x
    hi = sq.astype(BF16)
    lo = (sq - hi.astype(F32)).astype(BF16)
    return _dot(hi, bmat) + _dot(lo, bmat)


def _memkv_body(mem_ref, g_ref, wk_ref, wv_ref, gk_ref, bmat_ref, mk_ref, mv_ref):
    m = _rms(mem_ref[...], g_ref[...]).astype(BF16)
    kk = _dot(m, wk_ref[...])
    bmat = bmat_ref[...]
    for c in range(MEM_WIDTH // LANES):
        kc = kk[:, c * LANES:(c + 1) * LANES]
        ms = _head_mean_sq(kc, bmat)
        mk_ref[:, c * LANES:(c + 1) * LANES] = kc * lax.rsqrt(ms + EPS) * gk_ref[...]
    mv_ref[...] = _dot(m, wv_ref[...])


def _memory_kv(mem, mem_norm, w_mem_k, w_mem_v, mem_k_norm, bmat):
    m = mem.shape[0]
    return pl.pallas_call(
        _memkv_body,
        grid=(1,),
        in_specs=[_full((m, D_MODEL)), _full((1, D_MODEL)), _full((D_MODEL, MEM_WIDTH)),
                  _full((D_MODEL, MEM_WIDTH)), _full((1, LANES)), _full((LANES, LANES))],
        out_specs=[_full((m, MEM_WIDTH)), _full((m, MEM_WIDTH))],
        out_shape=[jax.ShapeDtypeStruct((m, MEM_WIDTH), F32)] * 2,
        compiler_params=_params(("arbitrary",)),
        name="memory_kv",
    )(mem, mem_norm.reshape(1, D_MODEL), w_mem_k.astype(BF16), w_mem_v.astype(BF16),
      jnp.tile(mem_k_norm, 2).reshape(1, LANES), bmat)


def _rope(x, c, sa, sb, shift):
    return (x * c + pltpu.roll(x, LANES - shift, axis=1) * sa + pltpu.roll(x, shift, axis=1) * sb)


def _proj_body(x_ref, cq_ref, saq_ref, sbq_ref, ci_ref, sai_ref, sbi_ref, g1_ref, w_ref,
               gq_ref, gk_ref, gm_ref, bmat_ref,
               u_ref, qx_ref, k_ref, kb_ref, v_ref, vt_ref, ki_ref, kir_ref, qi_ref, wi_ref, qm_ref):
    nb = _rms(x_ref[...], g1_ref[...]).astype(BF16)
    proj = _dot(nb, w_ref[...])
    u_ref[...] = proj[:, COL_U:COL_U + POOL_WIDTH]

    bmat = bmat_ref[...]
    cq, saq, sbq = cq_ref[...], saq_ref[...], sbq_ref[...]
    half_q = ROT_DIM // 2

    def head_norm(xc, g):
        ms = _head_mean_sq(xc, bmat)
        return xc * lax.rsqrt(ms + EPS) * g

    lane = lax.broadcasted_iota(I32, (1, LANES), 1)
    low = lane < HEAD_DIM
    n_pair = N_HEADS // 2
    for g in range(n_pair):
        qc = proj[:, COL_Q + g * LANES:COL_Q + (g + 1) * LANES]
        qc = _rope(head_norm(qc, gq_ref[...]), cq, saq, sbq, half_q) * Q_SCALE
        qx_ref[g] = jnp.where(low, qc, 0.0).astype(BF16)
        qx_ref[n_pair + g] = jnp.where(low, 0.0, qc).astype(BF16)

    kc = _rope(head_norm(proj[:, COL_K:COL_K + LANES], gk_ref[...]), cq, saq, sbq, half_q)
    k_ref[...] = kc
    kb_ref[...] = kc.astype(BF16)

    vc = proj[:, COL_V:COL_V + LANES]
    v_ref[...] = vc
    vt_ref[...] = vc.T.astype(BF16)

    ci, sai, sbi = ci_ref[...], sai_ref[...], sbi_ref[...]
    half_i = IDX_ROT_DIM // 2
    for c in range(IDX_HEADS * IDX_DIM // LANES):
        sl = slice(c * LANES, (c + 1) * LANES)
        qic = _rope(proj[:, COL_QI + c * LANES:COL_QI + (c + 1) * LANES], ci, sai, sbi, half_i)
        qi_ref[:, sl] = (qic * (IDX_DIM ** -0.5)).astype(BF16)
        kic = _rope(proj[:, COL_KI + c * LANES:COL_KI + (c + 1) * LANES], ci, sai, sbi, half_i)
        kir_ref[:, sl] = kic.astype(BF16)
        if c == 0:
            ki_ref[...] = kic[:, :IDX_DIM]

    wi_ref[...] = proj[:, COL_WI:COL_WI + LANES] * (IDX_HEADS ** -0.5)

    for c in range(MEM_WIDTH // LANES):
        qmc = head_norm(proj[:, COL_QM + c * LANES:COL_QM + (c + 1) * LANES], gm_ref[...])
        qm_ref[:, c * LANES:(c + 1) * LANES] = (qmc * (HEAD_DIM ** -0.5)).astype(BF16)


def _rope_tables(pos, rot_dim, period):
    half = rot_dim // 2
    inv = ROPE_THETA ** (-jnp.arange(half, dtype=F32) / half)
    ang = pos.astype(F32)[:, None] * inv[None, :]
    cos, sin = jnp.cos(ang), jnp.sin(ang)
    lane = np.arange(LANES) % period
    first = lane < half
    second = (lane >= half) & (lane < rot_dim)
    idx = np.where(first, lane, np.where(second, lane - half, 0))
    c = jnp.where(jnp.asarray(first | second)[None, :], cos[:, idx], 1.0)
    sa = jnp.where(jnp.asarray(first)[None, :], -sin[:, idx], 0.0)
    sb = jnp.where(jnp.asarray(second)[None, :], sin[:, idx], 0.0)
    return c, sa, sb


def _prep_w_in(w_in):
    cuts = np.cumsum(SPLITS)[:-1]
    wu, wq, wk, wv, wqi, wki, wwi, wqm = jnp.split(w_in, [int(c) for c in cuts], axis=1)
    order = [h for g in range(N_HEADS // 2) for h in (g, N_HEADS // 2 + g)]
    wq = wq.reshape(D_MODEL, N_HEADS, HEAD_DIM)[:, order, :].reshape(D_MODEL, N_HEADS * HEAD_DIM)
    wki = jnp.tile(wki, (1, IDX_HEADS))
    wwi = jnp.pad(wwi, ((0, 0), (0, LANES - IDX_HEADS)))
    return jnp.concatenate([wu, wq, wk, wv, wqi, wki, wwi, wqm], axis=1).astype(BF16)


def _project(x, pos, norm1, wp, q_norm, k_norm, mem_q_norm, bmat, tm):
    t = x.shape[0]
    cq, saq, sbq = _rope_tables(pos, ROT_DIM, HEAD_DIM)
    ci, sai, sbi = _rope_tables(pos, IDX_ROT_DIM, IDX_DIM)
    row = lambda w: pl.BlockSpec((tm, w), lambda i: (i, 0))
    tab = row(LANES)
    out_shape = [
        jax.ShapeDtypeStruct((t, POOL_WIDTH), F32),
        jax.ShapeDtypeStruct((N_HEADS, t, LANES), BF16),
        jax.ShapeDtypeStruct((t, LANES), F32),
        jax.ShapeDtypeStruct((t, LANES), BF16),
        jax.ShapeDtypeStruct((t, LANES), F32),
        jax.ShapeDtypeStruct((LANES, t), BF16),
        jax.ShapeDtypeStruct((t, IDX_DIM), F32),
        jax.ShapeDtypeStruct((t, IDX_HEADS * IDX_DIM), BF16),
        jax.ShapeDtypeStruct((t, IDX_HEADS * IDX_DIM), BF16),
        jax.ShapeDtypeStruct((t, LANES), F32),
        jax.ShapeDtypeStruct((t, MEM_WIDTH), BF16),
    ]
    out_specs = [
        row(POOL_WIDTH),
        pl.BlockSpec((N_HEADS, tm, LANES), lambda i: (0, i, 0)),
        row(LANES), row(LANES), row(LANES),
        pl.BlockSpec((LANES, tm), lambda i: (0, i)),
        row(IDX_DIM), row(IDX_HEADS * IDX_DIM), row(IDX_HEADS * IDX_DIM), row(LANES), row(MEM_WIDTH),
    ]
    return pl.pallas_call(
        _proj_body,
        grid=(t // tm,),
        in_specs=[row(D_MODEL), tab, tab, tab, tab, tab, tab, _full((1, D_MODEL)),
                  _resident((D_MODEL, PROJ_W)), _full((1, LANES)), _full((1, LANES)),
                  _full((1, LANES)), _full((LANES, LANES))],
        out_specs=out_specs,
        out_shape=out_shape,
        compiler_params=_params(("arbitrary",)),
        name="input_projection",
    )(x, cq, saq, sbq, ci, sai, sbi, norm1.reshape(1, D_MODEL), wp,
      jnp.tile(q_norm, 2).reshape(1, LANES), jnp.tile(k_norm, 2).reshape(1, LANES),
      jnp.tile(mem_q_norm, 2).reshape(1, LANES), bmat)


def _pool_mem_body(u_ref, hist_ref, pw_ref, ps_ref, qm_ref, mk_ref, mv_ref,
                   apool_ref, amem_ref, carry_ref, *, tiles_per_seq, pos0):
    j = pl.program_id(0)
    tm = u_ref.shape[0]
    tile_in_seq = j % tiles_per_seq

    @pl.when(tile_in_seq == 0)
    def _():
        carry_ref[...] = hist_ref[0]

    u = u_ref[...]
    ext = jnp.concatenate([carry_ref[...], u], axis=0)
    carry_ref[...] = u[tm - 2 * SUBLANES:, :]
    s2 = ext + pltpu.roll(ext, 1, axis=0)
    s4 = s2 + pltpu.roll(s2, 2, axis=0)
    s8 = s4 + pltpu.roll(s4, 4, axis=0)
    s16 = s8 + pltpu.roll(s8, 8, axis=0)
    hs = 2 * SUBLANES
    gw = POOL_WIDTH // POOL_GROUPS
    lane = lax.broadcasted_iota(I32, (1, POOL_WIDTH), 1)
    win_sum = jnp.where(lane < gw, s2[hs:], jnp.where(lane < 2 * gw, s4[hs:],
                        jnp.where(lane < 3 * gw, s8[hs:], s16[hs:])))
    win = jnp.where(lane < gw, POOL_WINDOWS[0], jnp.where(lane < 2 * gw, POOL_WINDOWS[1],
                    jnp.where(lane < 3 * gw, POOL_WINDOWS[2], POOL_WINDOWS[3])))
    pos = pos0 + tile_in_seq * tm + lax.broadcasted_iota(I32, (tm, 1), 0)
    cnt = jnp.minimum(win, pos + 1).astype(F32)
    z = win_sum / cnt - u
    y = _dot(z.astype(BF16), pw_ref[...]) * ps_ref[...]
    apool_ref[...] = y.astype(BF16)

    qm = qm_ref[...]
    lane_m = lax.broadcasted_iota(I32, (1, MEM_WIDTH), 1)
    mk = mk_ref[0].astype(BF16)
    mv = mv_ref[0].astype(BF16)
    zero = jnp.zeros_like(qm)
    qs = jnp.concatenate(
        [jnp.where((lane_m >= h * HEAD_DIM) & (lane_m < (h + 1) * HEAD_DIM), qm, zero)
         for h in range(MEM_HEADS)], axis=0)
    logits = _dot_nt(qs, mk)
    mx = jnp.max(logits, axis=-1, keepdims=True)
    p = jnp.exp(logits - mx)
    den = jnp.sum(p, axis=-1, keepdims=True)
    o = _dot(p.astype(BF16), mv) / den
    out = jnp.zeros((tm, MEM_WIDTH), F32)
    for h in range(MEM_HEADS):
        oh = o[h * tm:(h + 1) * tm]
        out = jnp.where((lane_m >= h * HEAD_DIM) & (lane_m < (h + 1) * HEAD_DIM), oh, out)
    amem_ref[...] = out.astype(BF16)


def _pool_mem(u, hist, pw_bd, pool_scale, qm, mk, mv, tm, tiles_per_seq, pos0):
    t = u.shape[0]
    seq = lambda j: j // tiles_per_seq
    return pl.pallas_call(
        functools.partial(_pool_mem_body, tiles_per_seq=tiles_per_seq, pos0=pos0),
        grid=(t // tm,),
        in_specs=[pl.BlockSpec((tm, POOL_WIDTH), lambda j: (j, 0)),
                  pl.BlockSpec((1, 2 * SUBLANES, POOL_WIDTH), lambda j: (seq(j), 0, 0)),
                  _full((POOL_WIDTH, POOL_WIDTH)), _full((1, POOL_WIDTH)),
                  pl.BlockSpec((tm, MEM_WIDTH), lambda j: (j, 0)),
                  pl.BlockSpec((1, MEM_TOKENS, MEM_WIDTH), lambda j: (seq(j) % mk.shape[0], 0, 0)),
                  pl.BlockSpec((1, MEM_TOKENS, MEM_WIDTH), lambda j: (seq(j) % mk.shape[0], 0, 0))],
        out_specs=[pl.BlockSpec((tm, POOL_WIDTH), lambda j: (j, 0)),
                   pl.BlockSpec((tm, MEM_WIDTH), lambda j: (j, 0))],
        out_shape=[jax.ShapeDtypeStruct((t, POOL_WIDTH), BF16),
                   jax.ShapeDtypeStruct((t, MEM_WIDTH), BF16)],
        scratch_shapes=[pltpu.VMEM((2 * SUBLANES, POOL_WIDTH), F32)],
        compiler_params=_params(("arbitrary",)),
        name="pool_and_memory_attention",
    )(u, hist, pw_bd, pool_scale.reshape(1, POOL_WIDTH), qm, mk, mv)


def _order_key(score):
    bits = pltpu.bitcast(score + 0.0, I32)
    return bits ^ ((bits >> 31) & 0x7FFFFFFF)


def _sublane_total(x):
    x = x + pltpu.roll(x, 4, axis=0)
    x = x + pltpu.roll(x, 2, axis=0)
    return x + pltpu.roll(x, 1, axis=0)


def _dsa_body(lim_ref, qx_ref, qi_ref, wi_ref, kb_ref, vt_ref, kir_ref, out_ref,
              key_ref, acc_ref, *, causal, n_tiles_static, top_k):
    qb = pl.program_id(0)
    tk = KTILE
    if causal:
        n_tiles = ((qb + 1) * QBLK + tk - 1) // tk
    else:
        n_tiles = n_tiles_static
    limit = lim_ref[0]

    qi = qi_ref[...]
    lane_i = lax.broadcasted_iota(I32, (1, IDX_HEADS * IDX_DIM), 1)
    zero_qi = jnp.zeros_like(qi)
    qi_rows = jnp.concatenate(
        [jnp.where((lane_i >= h * IDX_DIM) & (lane_i < (h + 1) * IDX_DIM), qi, zero_qi)
         for h in range(IDX_HEADS)], axis=0)
    w_t = wi_ref[...].T
    n_cls = TOPK_MAX

    def score_tile(i, cls_max):
        r0 = pl.multiple_of(i * tk, tk)
        s = _dot_nt(kir_ref[pl.ds(r0, tk), :], qi_rows)
        score = jnp.zeros((tk, QBLK), F32)
        for h in range(IDX_HEADS):
            score = score + jnp.maximum(s[:, h * QBLK:(h + 1) * QBLK], 0.0) * w_t[h:h + 1, :]
        kidx = r0 + lax.broadcasted_iota(I32, (tk, 1), 0)
        key = jnp.where(kidx < limit, _order_key(score), KEY_NEG_INF)
        key_ref[pl.ds(r0, tk), :] = key
        for c in range(tk // n_cls):
            cls_max = jnp.maximum(cls_max, key[c * n_cls:(c + 1) * n_cls])
        return cls_max

    cls_max = lax.fori_loop(0, n_tiles, score_tile,
                            jnp.full((n_cls, QBLK), KEY_NEG_INF, I32))
    lo0 = jnp.min(cls_max, axis=0, keepdims=True)
    hi0 = jnp.max(cls_max, axis=0, keepdims=True) + 1
    few = limit <= top_k
    rep = lambda x: jnp.broadcast_to(x, (SUBLANES, QBLK))
    lo0 = rep(jnp.where(few, KEY_NEG_INF + 1, lo0))
    hi0 = rep(jnp.where(few, KEY_NEG_INF + 2, hi0))

    def count_ge(thr):
        def body(i, acc):
            r0 = pl.multiple_of(i * tk, tk)
            keys = key_ref[pl.ds(r0, tk), :].reshape(tk // SUBLANES, SUBLANES, QBLK)
            return acc + jnp.sum((keys >= thr[None]).astype(I32), axis=0)
        acc = lax.fori_loop(0, n_tiles, body, jnp.zeros((SUBLANES, QBLK), I32))
        return _sublane_total(acc)

    def search_cond(st):
        step, lo, hi, c_lo, c_hi = st
        open_ = (c_lo != top_k) & (hi > lo + 1)
        return (step < MAX_SEARCH_STEPS) & (jnp.max(open_.astype(I32)) > 0)

    def search_step(st):
        step, lo, hi, c_lo, c_hi = st
        open_ = (c_lo != top_k) & (hi > lo + 1)
        mid = (lo >> 1) + (hi >> 1) + (lo & hi & 1)
        c = count_ge(mid)
        up = open_ & (c >= top_k)
        dn = open_ & (c < top_k)
        return (step + 1, jnp.where(up, mid, lo), jnp.where(dn, mid, hi),
                jnp.where(up, c, c_lo), jnp.where(dn, c, c_hi))

    big = jnp.full((SUBLANES, QBLK), 2 * top_k, I32)
    c_lo0 = jnp.where(rep(few), top_k, big)
    _, thr, _, c_lo, c_hi = lax.while_loop(
        search_cond, search_step, (jnp.int32(0), lo0, hi0, c_lo0, jnp.zeros_like(big)))

    tied = c_lo != top_k
    need = jnp.where(tied, top_k - c_hi, jnp.int32(2 ** 30))

    @pl.when(jnp.max(tied.astype(I32)) > 0)
    def _():
        sub = lax.broadcasted_iota(I32, (SUBLANES, QBLK), 0)

        def fix(v, seen):
            r0 = pl.multiple_of(v * SUBLANES, SUBLANES)
            keys = key_ref[pl.ds(r0, SUBLANES), :]
            eq = (keys == thr).astype(I32)
            pre = eq
            for sft in (1, 2, 4):
                pre = pre + jnp.where(sub >= sft, pltpu.roll(pre, sft, axis=0), 0)
            rank = seen + pre - eq
            key_ref[pl.ds(r0, SUBLANES), :] = jnp.where((eq > 0) & (rank >= need), thr - 1, keys)
            return seen + _sublane_total(eq)

        lax.fori_loop(0, n_tiles * (tk // SUBLANES), fix, jnp.zeros((SUBLANES, QBLK), I32))

    qx = qx_ref[...].reshape(N_HEADS * QBLK, LANES)
    thr_row = thr[0:1, :]
    acc_ref[...] = jnp.zeros_like(acc_ref)

    def attend(i, st):
        m, l = st
        r0 = pl.multiple_of(i * tk, tk)
        logit = _dot_nt(kb_ref[pl.ds(r0, tk), :], qx)
        sel = key_ref[pl.ds(r0, tk), :] >= thr_row
        m_new = jnp.maximum(m, jnp.max(logit, axis=0, keepdims=True))
        alpha = jnp.exp2(m - m_new)
        ps = []
        for h in range(N_HEADS):
            sl = slice(h * QBLK, (h + 1) * QBLK)
            ps.append(jnp.where(sel, jnp.exp2(logit[:, sl] - m_new[:, sl]), 0.0))
        p = jnp.concatenate(ps, axis=1)
        l = alpha * l + jnp.sum(p, axis=0, keepdims=True)
        pv = _dot(vt_ref[:, pl.ds(r0, tk)], p.astype(BF16))
        acc_ref[...] = acc_ref[...] * alpha + pv
        return m_new, l

    m0 = jnp.full((1, N_HEADS * QBLK), -1e30, F32)
    l0 = jnp.zeros((1, N_HEADS * QBLK), F32)
    _, l = lax.fori_loop(0, n_tiles, attend, (m0, l0))

    o = acc_ref[...] / l
    hpk = N_HEADS // N_KV_HEADS
    for g in range(N_HEADS // 2):
        parts = []
        for h in (2 * g, 2 * g + 1):
            jkv = h // hpk
            parts.append(o[jkv * HEAD_DIM:(jkv + 1) * HEAD_DIM, h * QBLK:(h + 1) * QBLK])
        blk = jnp.concatenate(parts, axis=0)
        out_ref[:, g * LANES:(g + 1) * LANES] = blk.T.astype(BF16)


def _dsa(limits, qx, qi, wi, kb, vt, kir, *, causal, top_k):
    nb, _, t, _ = qx.shape
    lp = kb.shape[1]
    nq = t // QBLK
    body = functools.partial(_dsa_body, causal=causal, n_tiles_static=lp // KTILE, top_k=top_k)
    bq = lambda i: (i // nq, i % nq)
    return pl.pallas_call(
        body,
        grid=(nb * nq,),
        in_specs=[
            pl.BlockSpec((1, 1, QBLK), lambda i: (i, 0, 0)),
            pl.BlockSpec((None, N_HEADS, QBLK, LANES), lambda i: (bq(i)[0], 0, bq(i)[1], 0)),
            pl.BlockSpec((None, QBLK, IDX_HEADS * IDX_DIM), lambda i: (bq(i)[0], bq(i)[1], 0)),
            pl.BlockSpec((None, QBLK, LANES), lambda i: (bq(i)[0], bq(i)[1], 0)),
            pl.BlockSpec((None, lp, LANES), lambda i: (bq(i)[0], 0, 0), pipeline_mode=pl.Buffered(1)),
            pl.BlockSpec((None, LANES, lp), lambda i: (bq(i)[0], 0, 0), pipeline_mode=pl.Buffered(1)),
            pl.BlockSpec((None, lp, IDX_HEADS * IDX_DIM), lambda i: (bq(i)[0], 0, 0),
                         pipeline_mode=pl.Buffered(1)),
        ],
        out_specs=pl.BlockSpec((None, QBLK, N_HEADS * HEAD_DIM), lambda i: (bq(i)[0], bq(i)[1], 0)),
        out_shape=jax.ShapeDtypeStruct((nb, t, N_HEADS * HEAD_DIM), BF16),
        scratch_shapes=[pltpu.VMEM((lp, QBLK), I32), pltpu.VMEM((LANES, N_HEADS * QBLK), F32)],
        compiler_params=_params(("arbitrary",)),
        name="dsa_attention",
    )(limits, qx, qi, wi, kb, vt, kir)


def _ffn_body(x_ref, ap_ref, aa_ref, am_ref, wo_ref, g2_ref, wup_ref, cw_ref, cb_ref, wdn_ref,
              hist_ref, y_ref, cst_ref, carry_ref, acc_ref, *, tiles_per_seq):
    j = pl.program_id(0)
    tm = x_ref.shape[0]
    tile_in_seq = j % tiles_per_seq

    @pl.when(tile_in_seq == 0)
    def _():
        carry_ref[...] = hist_ref[0]

    mix = jnp.concatenate([ap_ref[...], aa_ref[...], am_ref[...]], axis=1)
    h = x_ref[...] + _dot(mix, wo_ref[...])
    nb = _rms(h, g2_ref[...]).astype(BF16)
    acc_ref[...] = h
    n_chunk = D_FF // FFN_CHUNK
    for c in range(n_chunk):
        conv = []
        for part in range(2):
            c0 = part * D_FF + c * FFN_CHUNK
            cols = slice(c0, c0 + FFN_CHUNK)
            up = _dot(nb, wup_ref[:, cols])
            ext = jnp.concatenate([carry_ref[:, cols], up], axis=0)
            carry_ref[:, cols] = up[tm - SUBLANES:, :]

            @pl.when(tile_in_seq == tiles_per_seq - 1)
            def _():
                cst_ref[0, :, cols] = up[tm - (CONV_W - 1):, :]

            cv = (cb_ref[:, cols] + ext[SUBLANES:] * cw_ref[2:3, cols]
                  + pltpu.roll(ext, 1, axis=0)[SUBLANES:] * cw_ref[1:2, cols]
                  + pltpu.roll(ext, 2, axis=0)[SUBLANES:] * cw_ref[0:1, cols])
            conv.append(cv)
        gate, val = conv
        act = gate / (1.0 + jnp.exp(-gate)) * val
        acc_ref[...] += _dot(act.astype(BF16), wdn_ref[c * FFN_CHUNK:(c + 1) * FFN_CHUNK, :])
    y_ref[...] = acc_ref[...]


def _out_ffn(x, a_pool, a_attn, a_mem, w_out, norm2, w_up, conv_w, conv_b, w_down, hist, tm,
             tiles_per_seq):
    t = x.shape[0]
    n_seq = t // (tm * tiles_per_seq)
    row = lambda w: pl.BlockSpec((tm, w), lambda j: (j, 0))
    seq = lambda j: j // tiles_per_seq
    return pl.pallas_call(
        functools.partial(_ffn_body, tiles_per_seq=tiles_per_seq),
        grid=(t // tm,),
        in_specs=[row(D_MODEL), row(POOL_WIDTH), row(N_HEADS * HEAD_DIM), row(MEM_WIDTH),
                  _resident((D_MODEL, D_MODEL)), _full((1, D_MODEL)),
                  _resident((D_MODEL, 2 * D_FF)), _full((CONV_W, 2 * D_FF)), _full((1, 2 * D_FF)),
                  _resident((D_FF, D_MODEL)),
                  pl.BlockSpec((1, SUBLANES, 2 * D_FF), lambda j: (seq(j), 0, 0))],
        out_specs=[row(D_MODEL),
                   pl.BlockSpec((1, CONV_W - 1, 2 * D_FF), lambda j: (seq(j), 0, 0))],
        out_shape=[jax.ShapeDtypeStruct((t, D_MODEL), F32),
                   jax.ShapeDtypeStruct((n_seq, CONV_W - 1, 2 * D_FF), F32)],
        scratch_shapes=[pltpu.VMEM((SUBLANES, 2 * D_FF), F32), pltpu.VMEM((tm, D_MODEL), F32)],
        compiler_params=_params(("arbitrary",)),
        name="out_proj_conv_ffn",
    )(x, a_pool, a_attn, a_mem, w_out.astype(BF16), norm2.reshape(1, D_MODEL), w_up.astype(BF16),
      conv_w, conv_b.reshape(1, 2 * D_FF), w_down.astype(BF16), hist)


def _pad_rows(a, rows, axis):
    pad = [(0, 0)] * a.ndim
    pad[axis] = (0, rows - a.shape[axis])
    return jnp.pad(a, pad)


def _layer(x, pos, pos0, keys_past, mk, mv, pool_hist, conv_hist, lw, *, causal, tm_proj, tm_pool,
           tm_ffn):
    b, t, _ = x.shape
    xf = x.reshape(b * t, D_MODEL)
    (u, qx, k, kb, v, vt, ki, kir, qi, wi, qm) = _project(
        xf, pos, lw['norm1'], lw['wp'], lw['q_norm'], lw['k_norm'], lw['mem_q_norm'], lw['bmat'],
        tm_proj)

    hist16 = jnp.pad(pool_hist, ((0, 0), (2 * SUBLANES - POOL_HIST, 0), (0, 0)))
    a_pool, a_mem = _pool_mem(u, hist16, lw['pw_bd'], lw['pool_scale'], qm, mk, mv, tm_pool,
                              t // tm_pool, pos0)

    if keys_past is None:
        l_keys = t
        top_k = min(TOPK_MAX, l_keys // 4)
        lp = -(-l_keys // KTILE) * KTILE
        kb_all = _pad_rows(kb.reshape(b, t, LANES), lp, 1)
        vt_all = _pad_rows(vt.reshape(LANES, b, t).transpose(1, 0, 2), lp, 2)
        kir_all = _pad_rows(kir.reshape(b, t, -1), lp, 1)
        limits = ((jnp.arange(t, dtype=I32) // CHUNK + 1) * CHUNK)
        limits = jnp.tile(limits.reshape(1, t // QBLK, 1, QBLK), (b, 1, 1, 1)).reshape(-1, 1, QBLK)
        qx_b = qx.reshape(N_HEADS, b, t, LANES).transpose(1, 0, 2, 3)
        qi_b = qi.reshape(b, t, -1)
        wi_b = wi.reshape(b, t, LANES)
        a_attn = _dsa(limits, qx_b, qi_b, wi_b, kb_all, vt_all, kir_all, causal=True, top_k=top_k)
        a_attn = a_attn.reshape(b * t, N_HEADS * HEAD_DIM)
    else:
        ck, cv, cki = keys_past
        p_len = ck.shape[1]
        l_keys = p_len + t
        top_k = min(TOPK_MAX, l_keys // 4)
        lp = -(-l_keys // KTILE) * KTILE
        kb_all = _pad_rows(jnp.concatenate([ck.astype(BF16), kb.reshape(b, t, LANES)], axis=1), lp, 1)
        v_new_t = vt.reshape(LANES, b, t).transpose(1, 0, 2)
        vt_all = _pad_rows(jnp.concatenate([cv.astype(BF16).transpose(0, 2, 1), v_new_t], axis=2), lp, 2)
        kir_all = _pad_rows(jnp.concatenate(
            [jnp.tile(cki.astype(BF16), (1, 1, IDX_HEADS)), kir.reshape(b, t, -1)], axis=1), lp, 1)
        reps = QBLK // t
        limits = jnp.full((b, 1, QBLK), l_keys, I32)
        qx_b = jnp.tile(qx.reshape(N_HEADS, b, t, LANES).transpose(1, 0, 2, 3), (1, 1, reps, 1))
        qi_b = jnp.tile(qi.reshape(b, t, -1), (1, reps, 1))
        wi_b = jnp.tile(wi.reshape(b, t, LANES), (1, reps, 1))
        a_attn = _dsa(limits, qx_b, qi_b, wi_b, kb_all, vt_all, kir_all, causal=False, top_k=top_k)
        a_attn = a_attn[:, :t].reshape(b * t, N_HEADS * HEAD_DIM)

    hist8 = jnp.pad(conv_hist, ((0, 0), (SUBLANES - (CONV_W - 1), 0), (0, 0)))
    y, conv_state = _out_ffn(xf, a_pool, a_attn, a_mem, lw['w_out'], lw['norm2'], lw['w_up'],
                             lw['conv_w'], lw['conv_b'], lw['w_down'], hist8, tm_ffn, t // tm_ffn)
    y = y.reshape(b, t, D_MODEL)
    k4 = k.reshape(b, t, N_KV_HEADS, HEAD_DIM)
    v4 = v.reshape(b, t, N_KV_HEADS, HEAD_DIM)
    ki3 = ki.reshape(b, t, IDX_DIM)
    u3 = u.reshape(b, t, POOL_WIDTH)
    pool_state = jnp.concatenate([pool_hist, u3], axis=1)[:, -POOL_HIST:]
    return y, k4, v4, ki3, pool_state, conv_state


def kernel(x_prompt, x_sample, mem_prompt, cache_k, cache_v, cache_kidx, cache_mem_k, cache_mem_v,
           state_pool, state_ffn_conv, norm1, w_in, q_norm, k_norm, pool_w, pool_scale, mem_norm,
           w_mem_k, w_mem_v, mem_q_norm, mem_k_norm, w_out, norm2, w_up, conv_w, conv_b, w_down):
    depth = norm1.shape[0]
    bp, sp, _ = x_prompt.shape
    bs, ts, _ = x_sample.shape
    p_len = cache_k.shape[2]
    blk = np.kron(np.eye(LANES // HEAD_DIM), np.ones((HEAD_DIM, HEAD_DIM))) / HEAD_DIM
    bmat = jnp.asarray(blk, BF16)
    xp, xs = x_prompt, x_sample
    p_states, s_states = [], []
    for l in range(depth):
        gw = POOL_WIDTH // POOL_GROUPS
        pw_bd = jnp.zeros((POOL_WIDTH, POOL_WIDTH), F32)
        for g in range(POOL_GROUPS):
            pw_bd = pw_bd.at[g * gw:(g + 1) * gw, g * gw:(g + 1) * gw].set(pool_w[l, g])
        lw = dict(norm1=norm1[l], wp=_prep_w_in(w_in[l]), q_norm=q_norm[l], k_norm=k_norm[l],
                  mem_q_norm=mem_q_norm[l], bmat=bmat, pw_bd=pw_bd.astype(BF16),
                  pool_scale=pool_scale[l], w_out=w_out[l], norm2=norm2[l], w_up=w_up[l],
                  conv_w=conv_w[l], conv_b=conv_b[l], w_down=w_down[l])

        mks, mvs = [], []
        for b in range(bp):
            mk_b, mv_b = _memory_kv(mem_prompt[b], mem_norm[l], w_mem_k[l], w_mem_v[l],
                                    mem_k_norm[l], bmat)
            mks.append(mk_b)
            mvs.append(mv_b)
        mk_p, mv_p = jnp.stack(mks), jnp.stack(mvs)

        xp, k_p, v_p, ki_p, pool_p, conv_p = _layer(
            xp, jnp.arange(sp), 0, None, mk_p, mv_p,
            jnp.zeros((bp, POOL_HIST, POOL_WIDTH), F32), jnp.zeros((bp, CONV_W - 1, 2 * D_FF), F32),
            lw, causal=True, tm_proj=512, tm_pool=512, tm_ffn=256)
        p_states.append((k_p, v_p, ki_p, mk_p.reshape(bp, MEM_TOKENS, MEM_HEADS, HEAD_DIM),
                         mv_p.reshape(bp, MEM_TOKENS, MEM_HEADS, HEAD_DIM), pool_p, conv_p))

        pos_s = jnp.tile(p_len + jnp.arange(ts), bs)
        xs, k_s, v_s, ki_s, pool_s, conv_s = _layer(
            xs, pos_s, p_len,
            (cache_k[l].reshape(bs, p_len, LANES), cache_v[l].reshape(bs, p_len, LANES), cache_kidx[l]),
            cache_mem_k[l].reshape(bs, MEM_TOKENS, MEM_WIDTH),
            cache_mem_v[l].reshape(bs, MEM_TOKENS, MEM_WIDTH),
            state_pool[l], state_ffn_conv[l], lw, causal=False, tm_proj=ts * bs, tm_pool=ts, tm_ffn=ts)
        s_states.append((k_s, v_s, ki_s, pool_s, conv_s))

    k_p, v_p, kidx_p, memk_p, memv_p, pool_p, conv_p = [jnp.stack(z) for z in zip(*p_states)]
    k_s, v_s, kidx_s, pool_s, conv_s = [jnp.stack(z) for z in zip(*s_states)]
    return (xp, xs, k_p, v_p, kidx_p, memk_p, memv_p, pool_p, conv_p, k_s, v_s, kidx_s, pool_s, conv_s)
```

```python
import functools
import math

import numpy as np
import jax
import jax.numpy as jnp
from jax import lax
from jax.experimental import pallas as pl
from jax.experimental.pallas import tpu as pltpu

F32, BF16, I32 = jnp.float32, jnp.bfloat16, jnp.int32

D_MODEL = 1024
CHUNK = 64
HEAD_DIM = 64
N_HEADS = 8
N_KV_HEADS = 2
IDX_HEADS = 8
IDX_DIM = 32
TOPK_MAX = 256
MEM_TOKENS = 256
MEM_HEADS = 4
MEM_WIDTH = MEM_HEADS * HEAD_DIM
POOL_WIDTH = 256
POOL_GROUPS = 4
POOL_WINDOWS = (2, 4, 8, 16)
POOL_HIST = 15
D_FF = 2816
CONV_W = 3
ROPE_THETA = 500000.0
ROT_DIM = HEAD_DIM // 4
IDX_ROT_DIM = IDX_DIM // 4
EPS = 1e-6
SPLITS = (POOL_WIDTH, N_HEADS * HEAD_DIM, N_KV_HEADS * HEAD_DIM, N_KV_HEADS * HEAD_DIM,
          IDX_HEADS * IDX_DIM, IDX_DIM, IDX_HEADS, MEM_WIDTH)

LANES = 128
SUBLANES = 8
VMEM_LIMIT = 56 * 1024 * 1024

COL_U = 0
COL_Q = 256
COL_K = 768
COL_V = 896
COL_QI = 1024
COL_KI = 1280
COL_WI = 1536
COL_QM = 1664
PROJ_W = 1920

QBLK = 128
KTILE = 512
LOG2E = 1.4426950408889634
Q_SCALE = HEAD_DIM ** -0.5 * LOG2E
FFN_CHUNK = 256
KEY_NEG_INF = -2139095041
MAX_SEARCH_STEPS = 40
MIN_SEARCH_STEPS = 12
VT_ROWS = 144
FAST_SHIFT_LIMIT = 60.0


def _dot(a, b):
    return jnp.dot(a, b, preferred_element_type=F32)


def _dot_nt(a, b):
    return lax.dot_general(a, b, (((1,), (1,)), ((), ())), preferred_element_type=F32)


def _full(shape):
    n = len(shape)
    return pl.BlockSpec(shape, lambda *_: (0,) * n)


def _resident(shape):
    n = len(shape)
    return pl.BlockSpec(shape, lambda *_: (0,) * n, pipeline_mode=pl.Buffered(1))


def _params(sem):
    return pltpu.CompilerParams(dimension_semantics=sem, vmem_limit_bytes=VMEM_LIMIT)


def _rms(x, g):
    ms = jnp.mean(x * x, axis=-1, keepdims=True)
    return x * lax.rsqrt(ms + EPS) * g


def _head_mean_sq(x, bmat):
    sq = x * x
    hi = sq.astype(BF16)
    lo = (sq - hi.astype(F32)).astype(BF16)
    return _dot(hi, bmat) + _dot(lo, bmat)


def _memkv_body(mem_ref, g_ref, wk_ref, wv_ref, gk_ref, bmat_ref, mk_ref, mv_ref):
    m = _rms(mem_ref[...], g_ref[...]).astype(BF16)
    kk = _dot(m, wk_ref[...])
    bmat = bmat_ref[...]
    for c in range(MEM_WIDTH // LANES):
        kc = kk[:, c * LANES:(c + 1) * LANES]
        ms = _head_mean_sq(kc, bmat)
        mk_ref[:, c * LANES:(c + 1) * LANES] = kc * lax.rsqrt(ms + EPS) * gk_ref[...]
    mv_ref[...] = _dot(m, wv_ref[...])


def _memory_kv(mem, mem_norm, w_mem_k, w_mem_v, mem_k_norm, bmat):
    m = mem.shape[0]
    return pl.pallas_call(
        _memkv_body,
        grid=(1,),
        in_specs=[_full((m, D_MODEL)), _full((1, D_MODEL)), _full((D_MODEL, MEM_WIDTH)),
                  _full((D_MODEL, MEM_WIDTH)), _full((1, LANES)), _full((LANES, LANES))],
        out_specs=[_full((m, MEM_WIDTH)), _full((m, MEM_WIDTH))],
        out_shape=[jax.ShapeDtypeStruct((m, MEM_WIDTH), F32)] * 2,
        compiler_params=_params(("arbitrary",)),
        name="memory_kv",
    )(mem, mem_norm.reshape(1, D_MODEL), w_mem_k.astype(BF16), w_mem_v.astype(BF16),
      jnp.tile(mem_k_norm, 2).reshape(1, LANES), bmat)


def _rope(x, c, sa, sb, shift):
    return (x * c + pltpu.roll(x, LANES - shift, axis=1) * sa + pltpu.roll(x, shift, axis=1) * sb)


def _proj_body(x_ref, cq_ref, saq_ref, sbq_ref, ci_ref, sai_ref, sbi_ref, g1_ref, w_ref,
               gq_ref, gk_ref, gm_ref, bmat_ref,
               u_ref, qx_ref, k_ref, kb_ref, v_ref, vt_ref, ki_ref, kir_ref, qi_ref, wi_ref, qm_ref):
    nb = _rms(x_ref[...], g1_ref[...]).astype(BF16)
    proj = _dot(nb, w_ref[...])
    u_ref[...] = proj[:, COL_U:COL_U + POOL_WIDTH]

    bmat = bmat_ref[...]
    cq, saq, sbq = cq_ref[...], saq_ref[...], sbq_ref[...]
    half_q = ROT_DIM // 2

    def head_norm(xc, g):
        ms = _head_mean_sq(xc, bmat)
        return xc * lax.rsqrt(ms + EPS) * g

    lane = lax.broadcasted_iota(I32, (1, LANES), 1)
    low = lane < HEAD_DIM
    n_pair = N_HEADS // 2
    for g in range(n_pair):
        qc = proj[:, COL_Q + g * LANES:COL_Q + (g + 1) * LANES]
        qc = _rope(head_norm(qc, gq_ref[...]), cq, saq, sbq, half_q) * Q_SCALE
        qx_ref[g] = jnp.where(low, qc, 0.0).astype(BF16)
        qx_ref[n_pair + g] = jnp.where(low, 0.0, qc).astype(BF16)

    kc = _rope(head_norm(proj[:, COL_K:COL_K + LANES], gk_ref[...]), cq, saq, sbq, half_q)
    k_ref[...] = kc
    kb_ref[...] = kc.astype(BF16)

    vc = proj[:, COL_V:COL_V + LANES]
    v_ref[...] = vc
    vt_ref[0:LANES, :] = vc.T.astype(BF16)
    extra = lax.broadcasted_iota(I32, (VT_ROWS - LANES, vc.shape[0]), 0)
    vt_ref[LANES:VT_ROWS, :] = jnp.where(extra == 0, 1.0, 0.0).astype(BF16)

    ci, sai, sbi = ci_ref[...], sai_ref[...], sbi_ref[...]
    half_i = IDX_ROT_DIM // 2
    for c in range(IDX_HEADS * IDX_DIM // LANES):
        sl = slice(c * LANES, (c + 1) * LANES)
        qic = _rope(proj[:, COL_QI + c * LANES:COL_QI + (c + 1) * LANES], ci, sai, sbi, half_i)
        qi_ref[:, sl] = (qic * (IDX_DIM ** -0.5)).astype(BF16)
        kic = _rope(proj[:, COL_KI + c * LANES:COL_KI + (c + 1) * LANES], ci, sai, sbi, half_i)
        kir_ref[:, sl] = kic.astype(BF16)
        if c == 0:
            ki_ref[...] = kic[:, :IDX_DIM]

    wi_ref[...] = proj[:, COL_WI:COL_WI + LANES] * (IDX_HEADS ** -0.5)

    for c in range(MEM_WIDTH // LANES):
        qmc = head_norm(proj[:, COL_QM + c * LANES:COL_QM + (c + 1) * LANES], gm_ref[...])
        qm_ref[:, c * LANES:(c + 1) * LANES] = (qmc * (HEAD_DIM ** -0.5)).astype(BF16)


def _rope_tables(pos, rot_dim, period):
    half = rot_dim // 2
    inv = ROPE_THETA ** (-jnp.arange(half, dtype=F32) / half)
    ang = pos.astype(F32)[:, None] * inv[None, :]
    cos, sin = jnp.cos(ang), jnp.sin(ang)
    lane = np.arange(LANES) % period
    first = lane < half
    second = (lane >= half) & (lane < rot_dim)
    idx = np.where(first, lane, np.where(second, lane - half, 0))
    c = jnp.where(jnp.asarray(first | second)[None, :], cos[:, idx], 1.0)
    sa = jnp.where(jnp.asarray(first)[None, :], -sin[:, idx], 0.0)
    sb = jnp.where(jnp.asarray(second)[None, :], sin[:, idx], 0.0)
    return c, sa, sb


def _prep_w_in(w_in):
    cuts = np.cumsum(SPLITS)[:-1]
    wu, wq, wk, wv, wqi, wki, wwi, wqm = jnp.split(w_in, [int(c) for c in cuts], axis=1)
    order = [h for g in range(N_HEADS // 2) for h in (g, N_HEADS // 2 + g)]
    wq = wq.reshape(D_MODEL, N_HEADS, HEAD_DIM)[:, order, :].reshape(D_MODEL, N_HEADS * HEAD_DIM)
    wki = jnp.tile(wki, (1, IDX_HEADS))
    wwi = jnp.pad(wwi, ((0, 0), (0, LANES - IDX_HEADS)))
    return jnp.concatenate([wu, wq, wk, wv, wqi, wki, wwi, wqm], axis=1).astype(BF16)


def _project(x, pos, norm1, wp, q_norm, k_norm, mem_q_norm, bmat, tm):
    t = x.shape[0]
    cq, saq, sbq = _rope_tables(pos, ROT_DIM, HEAD_DIM)
    ci, sai, sbi = _rope_tables(pos, IDX_ROT_DIM, IDX_DIM)
    row = lambda w: pl.BlockSpec((tm, w), lambda i: (i, 0))
    tab = row(LANES)
    out_shape = [
        jax.ShapeDtypeStruct((t, POOL_WIDTH), F32),
        jax.ShapeDtypeStruct((N_HEADS, t, LANES), BF16),
        jax.ShapeDtypeStruct((t, LANES), F32),
        jax.ShapeDtypeStruct((t, LANES), BF16),
        jax.ShapeDtypeStruct((t, LANES), F32),
        jax.ShapeDtypeStruct((VT_ROWS, t), BF16),
        jax.ShapeDtypeStruct((t, IDX_DIM), F32),
        jax.ShapeDtypeStruct((t, IDX_HEADS * IDX_DIM), BF16),
        jax.ShapeDtypeStruct((t, IDX_HEADS * IDX_DIM), BF16),
        jax.ShapeDtypeStruct((t, LANES), F32),
        jax.ShapeDtypeStruct((t, MEM_WIDTH), BF16),
    ]
    out_specs = [
        row(POOL_WIDTH),
        pl.BlockSpec((N_HEADS, tm, LANES), lambda i: (0, i, 0)),
        row(LANES), row(LANES), row(LANES),
        pl.BlockSpec((VT_ROWS, tm), lambda i: (0, i)),
        row(IDX_DIM), row(IDX_HEADS * IDX_DIM), row(IDX_HEADS * IDX_DIM), row(LANES), row(MEM_WIDTH),
    ]
    return pl.pallas_call(
        _proj_body,
        grid=(t // tm,),
        in_specs=[row(D_MODEL), tab, tab, tab, tab, tab, tab, _full((1, D_MODEL)),
                  _resident((D_MODEL, PROJ_W)), _full((1, LANES)), _full((1, LANES)),
                  _full((1, LANES)), _full((LANES, LANES))],
        out_specs=out_specs,
        out_shape=out_shape,
        compiler_params=_params(("arbitrary",)),
        name="input_projection",
    )(x, cq, saq, sbq, ci, sai, sbi, norm1.reshape(1, D_MODEL), wp,
      jnp.tile(q_norm, 2).reshape(1, LANES), jnp.tile(k_norm, 2).reshape(1, LANES),
      jnp.tile(mem_q_norm, 2).reshape(1, LANES), bmat)


def _pool_mem_body(u_ref, hist_ref, pw_ref, ps_ref, qm_ref, mk_ref, mv_ref,
                   apool_ref, amem_ref, carry_ref, *, tiles_per_seq, pos0):
    j = pl.program_id(0)
    tm = u_ref.shape[0]
    tile_in_seq = j % tiles_per_seq

    @pl.when(tile_in_seq == 0)
    def _():
        carry_ref[...] = hist_ref[0]

    u = u_ref[...]
    ext = jnp.concatenate([carry_ref[...], u], axis=0)
    carry_ref[...] = u[tm - 2 * SUBLANES:, :]
    s2 = ext + pltpu.roll(ext, 1, axis=0)
    s4 = s2 + pltpu.roll(s2, 2, axis=0)
    s8 = s4 + pltpu.roll(s4, 4, axis=0)
    s16 = s8 + pltpu.roll(s8, 8, axis=0)
    hs = 2 * SUBLANES
    gw = POOL_WIDTH // POOL_GROUPS
    lane = lax.broadcasted_iota(I32, (1, POOL_WIDTH), 1)
    win_sum = jnp.where(lane < gw, s2[hs:], jnp.where(lane < 2 * gw, s4[hs:],
                        jnp.where(lane < 3 * gw, s8[hs:], s16[hs:])))
    win = jnp.where(lane < gw, POOL_WINDOWS[0], jnp.where(lane < 2 * gw, POOL_WINDOWS[1],
                    jnp.where(lane < 3 * gw, POOL_WINDOWS[2], POOL_WINDOWS[3])))
    pos = pos0 + tile_in_seq * tm + lax.broadcasted_iota(I32, (tm, 1), 0)
    cnt = jnp.minimum(win, pos + 1).astype(F32)
    z = win_sum / cnt - u
    y = _dot(z.astype(BF16), pw_ref[...]) * ps_ref[...]
    apool_ref[...] = y.astype(BF16)

    qm = qm_ref[...]
    lane_m = lax.broadcasted_iota(I32, (1, MEM_WIDTH), 1)
    mk = mk_ref[0].astype(BF16)
    mv = mv_ref[0].astype(BF16)
    zero = jnp.zeros_like(qm)
    qs = jnp.concatenate(
        [jnp.where((lane_m >= h * HEAD_DIM) & (lane_m < (h + 1) * HEAD_DIM), qm, zero)
         for h in range(MEM_HEADS)], axis=0)
    logits = _dot_nt(qs, mk)
    mx = jnp.max(logits, axis=-1, keepdims=True)
    p = jnp.exp(logits - mx)
    den = jnp.sum(p, axis=-1, keepdims=True)
    o = _dot(p.astype(BF16), mv) / den
    out = jnp.zeros((tm, MEM_WIDTH), F32)
    for h in range(MEM_HEADS):
        oh = o[h * tm:(h + 1) * tm]
        out = jnp.where((lane_m >= h * HEAD_DIM) & (lane_m < (h + 1) * HEAD_DIM), oh, out)
    amem_ref[...] = out.astype(BF16)


def _pool_mem(u, hist, pw_bd, pool_scale, qm, mk, mv, tm, tiles_per_seq, pos0):
    t = u.shape[0]
    seq = lambda j: j // tiles_per_seq
    return pl.pallas_call(
        functools.partial(_pool_mem_body, tiles_per_seq=tiles_per_seq, pos0=pos0),
        grid=(t // tm,),
        in_specs=[pl.BlockSpec((tm, POOL_WIDTH), lambda j: (j, 0)),
                  pl.BlockSpec((1, 2 * SUBLANES, POOL_WIDTH), lambda j: (seq(j), 0, 0)),
                  _full((POOL_WIDTH, POOL_WIDTH)), _full((1, POOL_WIDTH)),
                  pl.BlockSpec((tm, MEM_WIDTH), lambda j: (j, 0)),
                  pl.BlockSpec((1, MEM_TOKENS, MEM_WIDTH), lambda j: (seq(j) % mk.shape[0], 0, 0)),
                  pl.BlockSpec((1, MEM_TOKENS, MEM_WIDTH), lambda j: (seq(j) % mk.shape[0], 0, 0))],
        out_specs=[pl.BlockSpec((tm, POOL_WIDTH), lambda j: (j, 0)),
                   pl.BlockSpec((tm, MEM_WIDTH), lambda j: (j, 0))],
        out_shape=[jax.ShapeDtypeStruct((t, POOL_WIDTH), BF16),
                   jax.ShapeDtypeStruct((t, MEM_WIDTH), BF16)],
        scratch_shapes=[pltpu.VMEM((2 * SUBLANES, POOL_WIDTH), F32)],
        compiler_params=_params(("arbitrary",)),
        name="pool_and_memory_attention",
    )(u, hist, pw_bd, pool_scale.reshape(1, POOL_WIDTH), qm, mk, mv)


def _order_key(score):
    bits = pltpu.bitcast(score + 0.0, I32)
    return bits ^ ((bits >> 31) & 0x7FFFFFFF)


def _sublane_total(x):
    x = x + pltpu.roll(x, 4, axis=0)
    x = x + pltpu.roll(x, 2, axis=0)
    return x + pltpu.roll(x, 1, axis=0)


def _dsa_body(lim_ref, qx_ref, qi_ref, wi_ref, kb_ref, vt_ref, kir_ref, out_ref,
              key_ref, acc_ref, kmax_ref, *, causal, n_tiles_static, blocks_per_seq, top_k):
    qb = pl.program_id(0) % blocks_per_seq
    tk = KTILE
    if causal:
        n_tiles = ((qb + 1) * QBLK + tk - 1) // tk
    else:
        n_tiles = n_tiles_static
    limit = lim_ref[0]

    @pl.when(qb == 0)
    def _():
        ones = jnp.ones((LANES, LANES), BF16)

        def knorm(i, mx):
            kt = kb_ref[pl.ds(pl.multiple_of(i * tk, tk), tk), :].astype(F32)
            return jnp.maximum(mx, jnp.max(_dot((kt * kt).astype(BF16), ones), axis=0, keepdims=True))

        kmax_ref[...] = lax.fori_loop(0, n_tiles_static, knorm, jnp.zeros((1, LANES), F32))

    qi = qi_ref[...]
    lane_i = lax.broadcasted_iota(I32, (1, IDX_HEADS * IDX_DIM), 1)
    zero_qi = jnp.zeros_like(qi)
    qi_rows = jnp.concatenate(
        [jnp.where((lane_i >= h * IDX_DIM) & (lane_i < (h + 1) * IDX_DIM), qi, zero_qi)
         for h in range(IDX_HEADS)], axis=0)
    w_t = wi_ref[...].T
    n_cls = TOPK_MAX

    def score_tile(i, cls_max):
        r0 = pl.multiple_of(i * tk, tk)
        s = _dot_nt(kir_ref[pl.ds(r0, tk), :], qi_rows)
        score = jnp.zeros((tk, QBLK), F32)
        for h in range(IDX_HEADS):
            score = score + jnp.maximum(s[:, h * QBLK:(h + 1) * QBLK], 0.0) * w_t[h:h + 1, :]
        kidx = r0 + lax.broadcasted_iota(I32, (tk, 1), 0)
        key = jnp.where(kidx < limit, _order_key(score), KEY_NEG_INF)
        key_ref[pl.ds(r0, tk), :] = key
        for c in range(tk // n_cls):
            cls_max = jnp.maximum(cls_max, key[c * n_cls:(c + 1) * n_cls])
        return cls_max

    cls_max = lax.fori_loop(0, n_tiles, score_tile,
                            jnp.full((n_cls, QBLK), KEY_NEG_INF, I32))
    lo0 = jnp.min(cls_max, axis=0, keepdims=True)
    hi0 = jnp.max(cls_max, axis=0, keepdims=True) + 1
    few = limit <= top_k
    rep = lambda x: jnp.broadcast_to(x, (SUBLANES, QBLK))
    lo0 = rep(jnp.where(few, KEY_NEG_INF + 1, lo0))
    hi0 = rep(jnp.where(few, KEY_NEG_INF + 2, hi0))

    def count_ge(thr):
        def body(i, acc):
            r0 = pl.multiple_of(i * tk, tk)
            keys = key_ref[pl.ds(r0, tk), :].reshape(tk // SUBLANES, SUBLANES, QBLK)
            return acc + jnp.sum((keys >= thr[None]).astype(I32), axis=0)
        acc = lax.fori_loop(0, n_tiles, body, jnp.zeros((SUBLANES, QBLK), I32))
        return _sublane_total(acc)

    def is_open(lo, hi, c_lo):
        return (c_lo != top_k) & (hi > lo + 1)

    def search_step(st):
        lo, hi, c_lo, c_hi = st
        open_ = is_open(lo, hi, c_lo)
        mid = (lo >> 1) + (hi >> 1) + (lo & hi & 1)
        c = count_ge(mid)
        up = open_ & (c >= top_k)
        dn = open_ & (c < top_k)
        return (jnp.where(up, mid, lo), jnp.where(dn, mid, hi),
                jnp.where(up, c, c_lo), jnp.where(dn, c, c_hi))

    def search_cond(st):
        step, lo, hi, c_lo, _ = st
        any_open = jnp.max(is_open(lo, hi, c_lo)[0:1, :].astype(F32)) > 0.0
        return (step < MAX_SEARCH_STEPS) & any_open

    big = jnp.full((SUBLANES, QBLK), 2 * top_k, I32)
    c_lo0 = jnp.where(rep(few), top_k, big)
    st = lax.fori_loop(0, MIN_SEARCH_STEPS, lambda _, st: search_step(st),
                       (lo0, hi0, c_lo0, jnp.zeros_like(big)))
    _, thr, _, c_lo, c_hi = lax.while_loop(
        search_cond, lambda st: (st[0] + 1,) + search_step(st[1:]), (jnp.int32(MIN_SEARCH_STEPS),) + st)

    tied = c_lo != top_k
    need = jnp.where(tied, top_k - c_hi, jnp.int32(2 ** 30))

    @pl.when(jnp.max(tied.astype(I32)) > 0)
    def _():
        sub = lax.broadcasted_iota(I32, (SUBLANES, QBLK), 0)

        def fix(v, seen):
            r0 = pl.multiple_of(v * SUBLANES, SUBLANES)
            keys = key_ref[pl.ds(r0, SUBLANES), :]
            eq = (keys == thr).astype(I32)
            pre = eq
            for sft in (1, 2, 4):
                pre = pre + jnp.where(sub >= sft, pltpu.roll(pre, sft, axis=0), 0)
            rank = seen + pre - eq
            key_ref[pl.ds(r0, SUBLANES), :] = jnp.where((eq > 0) & (rank >= need), thr - 1, keys)
            return seen + _sublane_total(eq)

        lax.fori_loop(0, n_tiles * (tk // SUBLANES), fix, jnp.zeros((SUBLANES, QBLK), I32))

    qx = qx_ref[...].reshape(N_HEADS * QBLK, LANES)
    thr_row = thr[0:1, :]
    acc_ref[...] = jnp.zeros_like(acc_ref)

    qf = qx.astype(F32)
    qn2 = _dot_nt(jnp.ones((SUBLANES, LANES), BF16), (qf * qf).astype(BF16))[0:1, :]
    kmax2 = jnp.concatenate([kmax_ref[...]] * N_HEADS, axis=1)
    bound = jnp.sqrt(qn2 * kmax2) * 1.01
    fast = jnp.max(bound) <= FAST_SHIFT_LIMIT

    def finish(o):
        hpk = N_HEADS // N_KV_HEADS
        for g in range(N_HEADS // 2):
            parts = []
            for h in (2 * g, 2 * g + 1):
                jkv = h // hpk
                parts.append(o[jkv * HEAD_DIM:(jkv + 1) * HEAD_DIM, h * QBLK:(h + 1) * QBLK])
            blk = jnp.concatenate(parts, axis=0)
            out_ref[:, g * LANES:(g + 1) * LANES] = blk.T.astype(BF16)

    @pl.when(fast)
    def _():
        def attend(i, carry):
            r0 = pl.multiple_of(i * tk, tk)
            logit = _dot_nt(kb_ref[pl.ds(r0, tk), :], qx)
            sel = key_ref[pl.ds(r0, tk), :] >= thr_row
            ps = []
            for h in range(N_HEADS):
                sl = slice(h * QBLK, (h + 1) * QBLK)
                ps.append(jnp.where(sel, jnp.exp2(logit[:, sl] - bound[:, sl]), 0.0).astype(BF16))
            p = jnp.concatenate(ps, axis=1)
            acc_ref[...] += _dot(vt_ref[:, pl.ds(r0, tk)], p)
            return carry

        lax.fori_loop(0, n_tiles, attend, 0)
        acc = acc_ref[...]
        finish(acc[:LANES] / acc[LANES:LANES + 1])

    @pl.when(jnp.logical_not(fast))
    def _():
        def attend(i, m):
            r0 = pl.multiple_of(i * tk, tk)
            logit = _dot_nt(kb_ref[pl.ds(r0, tk), :], qx)
            sel = key_ref[pl.ds(r0, tk), :] >= thr_row
            m_new = jnp.maximum(m, jnp.max(logit, axis=0, keepdims=True))
            alpha = jnp.exp2(m - m_new)
            ps = []
            for h in range(N_HEADS):
                sl = slice(h * QBLK, (h + 1) * QBLK)
                ps.append(jnp.where(sel, jnp.exp2(logit[:, sl] - m_new[:, sl]), 0.0).astype(BF16))
            p = jnp.concatenate(ps, axis=1)
            acc_ref[...] = acc_ref[...] * alpha + _dot(vt_ref[:, pl.ds(r0, tk)], p)
            return m_new

        lax.fori_loop(0, n_tiles, attend, jnp.full((1, N_HEADS * QBLK), -1e30, F32))
        acc = acc_ref[...]
        finish(acc[:LANES] / acc[LANES:LANES + 1])


def _dsa(limits, qx, qi, wi, kb, vt, kir, *, causal, top_k):
    nb, _, t, _ = qx.shape
    lp = kb.shape[1]
    nq = t // QBLK
    body = functools.partial(_dsa_body, causal=causal, n_tiles_static=lp // KTILE,
                             blocks_per_seq=nq, top_k=top_k)
    bq = lambda i: (i // nq, i % nq)
    return pl.pallas_call(
        body,
        grid=(nb * nq,),
        in_specs=[
            pl.BlockSpec((1, 1, QBLK), lambda i: (i, 0, 0)),
            pl.BlockSpec((None, N_HEADS, QBLK, LANES), lambda i: (bq(i)[0], 0, bq(i)[1], 0)),
            pl.BlockSpec((None, QBLK, IDX_HEADS * IDX_DIM), lambda i: (bq(i)[0], bq(i)[1], 0)),
            pl.BlockSpec((None, QBLK, LANES), lambda i: (bq(i)[0], bq(i)[1], 0)),
            pl.BlockSpec((None, lp, LANES), lambda i: (bq(i)[0], 0, 0), pipeline_mode=pl.Buffered(1)),
            pl.BlockSpec((None, VT_ROWS, lp), lambda i: (bq(i)[0], 0, 0), pipeline_mode=pl.Buffered(1)),
            pl.BlockSpec((None, lp, IDX_HEADS * IDX_DIM), lambda i: (bq(i)[0], 0, 0),
                         pipeline_mode=pl.Buffered(1)),
        ],
        out_specs=pl.BlockSpec((None, QBLK, N_HEADS * HEAD_DIM), lambda i: (bq(i)[0], bq(i)[1], 0)),
        out_shape=jax.ShapeDtypeStruct((nb, t, N_HEADS * HEAD_DIM), BF16),
        scratch_shapes=[pltpu.VMEM((lp, QBLK), I32), pltpu.VMEM((VT_ROWS, N_HEADS * QBLK), F32),
                        pltpu.VMEM((1, LANES), F32)],
        compiler_params=_params(("arbitrary",)),
        name="dsa_attention",
    )(limits, qx, qi, wi, kb, vt, kir)


def _ffn_body(x_ref, ap_ref, aa_ref, am_ref, wo_ref, g2_ref, wup_ref, cw_ref, cb_ref, wdn_ref,
              hist_ref, y_ref, cst_ref, carry_ref, acc_ref, *, tiles_per_seq):
    j = pl.program_id(0)
    tm = x_ref.shape[0]
    tile_in_seq = j % tiles_per_seq

    @pl.when(tile_in_seq == 0)
    def _():
        carry_ref[...] = hist_ref[0]

    mix = jnp.concatenate([ap_ref[...], aa_ref[...], am_ref[...]], axis=1)
    h = x_ref[...] + _dot(mix, wo_ref[...])
    nb = _rms(h, g2_ref[...]).astype(BF16)
    acc_ref[...] = h
    n_chunk = D_FF // FFN_CHUNK
    for c in range(n_chunk):
        conv = []
        for part in range(2):
            c0 = part * D_FF + c * FFN_CHUNK
            cols = slice(c0, c0 + FFN_CHUNK)
            up = _dot(nb, wup_ref[:, cols])
            ext = jnp.concatenate([carry_ref[:, cols], up], axis=0)
            carry_ref[:, cols] = up[tm - SUBLANES:, :]

            @pl.when(tile_in_seq == tiles_per_seq - 1)
            def _():
                cst_ref[0, :, cols] = up[tm - (CONV_W - 1):, :]

            cv = (cb_ref[:, cols] + ext[SUBLANES:] * cw_ref[2:3, cols]
                  + pltpu.roll(ext, 1, axis=0)[SUBLANES:] * cw_ref[1:2, cols]
                  + pltpu.roll(ext, 2, axis=0)[SUBLANES:] * cw_ref[0:1, cols])
            conv.append(cv)
        gate, val = conv
        act = gate / (1.0 + jnp.exp(-gate)) * val
        acc_ref[...] += _dot(act.astype(BF16), wdn_ref[c * FFN_CHUNK:(c + 1) * FFN_CHUNK, :])
    y_ref[...] = acc_ref[...]


def _out_ffn(x, a_pool, a_attn, a_mem, w_out, norm2, w_up, conv_w, conv_b, w_down, hist, tm,
             tiles_per_seq):
    t = x.shape[0]
    n_seq = t // (tm * tiles_per_seq)
    row = lambda w: pl.BlockSpec((tm, w), lambda j: (j, 0))
    seq = lambda j: j // tiles_per_seq
    return pl.pallas_call(
        functools.partial(_ffn_body, tiles_per_seq=tiles_per_seq),
        grid=(t // tm,),
        in_specs=[row(D_MODEL), row(POOL_WIDTH), row(N_HEADS * HEAD_DIM), row(MEM_WIDTH),
                  _resident((D_MODEL, D_MODEL)), _full((1, D_MODEL)),
                  _resident((D_MODEL, 2 * D_FF)), _full((CONV_W, 2 * D_FF)), _full((1, 2 * D_FF)),
                  _resident((D_FF, D_MODEL)),
                  pl.BlockSpec((1, SUBLANES, 2 * D_FF), lambda j: (seq(j), 0, 0))],
        out_specs=[row(D_MODEL),
                   pl.BlockSpec((1, CONV_W - 1, 2 * D_FF), lambda j: (seq(j), 0, 0))],
        out_shape=[jax.ShapeDtypeStruct((t, D_MODEL), F32),
                   jax.ShapeDtypeStruct((n_seq, CONV_W - 1, 2 * D_FF), F32)],
        scratch_shapes=[pltpu.VMEM((SUBLANES, 2 * D_FF), F32), pltpu.VMEM((tm, D_MODEL), F32)],
        compiler_params=_params(("arbitrary",)),
        name="out_proj_conv_ffn",
    )(x, a_pool, a_attn, a_mem, w_out.astype(BF16), norm2.reshape(1, D_MODEL), w_up.astype(BF16),
      conv_w, conv_b.reshape(1, 2 * D_FF), w_down.astype(BF16), hist)


def _pad_rows(a, rows, axis):
    pad = [(0, 0)] * a.ndim
    pad[axis] = (0, rows - a.shape[axis])
    return jnp.pad(a, pad)


def _layer(x, pos, pos0, keys_past, mk, mv, pool_hist, conv_hist, lw, *, causal, tm_proj, tm_pool,
           tm_ffn):
    b, t, _ = x.shape
    xf = x.reshape(b * t, D_MODEL)
    (u, qx, k, kb, v, vt, ki, kir, qi, wi, qm) = _project(
        xf, pos, lw['norm1'], lw['wp'], lw['q_norm'], lw['k_norm'], lw['mem_q_norm'], lw['bmat'],
        tm_proj)

    hist16 = jnp.pad(pool_hist, ((0, 0), (2 * SUBLANES - POOL_HIST, 0), (0, 0)))
    a_pool, a_mem = _pool_mem(u, hist16, lw['pw_bd'], lw['pool_scale'], qm, mk, mv, tm_pool,
                              t // tm_pool, pos0)

    if keys_past is None:
        l_keys = t
        top_k = min(TOPK_MAX, l_keys // 4)
        lp = -(-l_keys // KTILE) * KTILE
        kb_all = _pad_rows(kb.reshape(b, t, LANES), lp, 1)
        vt_all = _pad_rows(vt.reshape(VT_ROWS, b, t).transpose(1, 0, 2), lp, 2)
        kir_all = _pad_rows(kir.reshape(b, t, -1), lp, 1)
        limits = ((jnp.arange(t, dtype=I32) // CHUNK + 1) * CHUNK)
        limits = jnp.tile(limits.reshape(1, t // QBLK, 1, QBLK), (b, 1, 1, 1)).reshape(-1, 1, QBLK)
        qx_b = qx.reshape(N_HEADS, b, t, LANES).transpose(1, 0, 2, 3)
        qi_b = qi.reshape(b, t, -1)
        wi_b = wi.reshape(b, t, LANES)
        a_attn = _dsa(limits, qx_b, qi_b, wi_b, kb_all, vt_all, kir_all, causal=True, top_k=top_k)
        a_attn = a_attn.reshape(b * t, N_HEADS * HEAD_DIM)
    else:
        ck, cv, cki = keys_past
        p_len = ck.shape[1]
        l_keys = p_len + t
        top_k = min(TOPK_MAX, l_keys // 4)
        lp = -(-l_keys // KTILE) * KTILE
        kb_all = _pad_rows(jnp.concatenate([ck.astype(BF16), kb.reshape(b, t, LANES)], axis=1), lp, 1)
        v_new_t = vt.reshape(VT_ROWS, b, t).transpose(1, 0, 2)
        ones_rows = jnp.zeros((b, VT_ROWS - LANES, p_len), BF16).at[:, 0, :].set(1.0)
        cv_t = jnp.concatenate([cv.astype(BF16).transpose(0, 2, 1), ones_rows], axis=1)
        vt_all = _pad_rows(jnp.concatenate([cv_t, v_new_t], axis=2), lp, 2)
        kir_all = _pad_rows(jnp.concatenate(
            [jnp.tile(cki.astype(BF16), (1, 1, IDX_HEADS)), kir.reshape(b, t, -1)], axis=1), lp, 1)
        reps = QBLK // t
        limits = jnp.full((b, 1, QBLK), l_keys, I32)
        qx_b = jnp.tile(qx.reshape(N_HEADS, b, t, LANES).transpose(1, 0, 2, 3), (1, 1, reps, 1))
        qi_b = jnp.tile(qi.reshape(b, t, -1), (1, reps, 1))
        wi_b = jnp.tile(wi.reshape(b, t, LANES), (1, reps, 1))
        a_attn = _dsa(limits, qx_b, qi_b, wi_b, kb_all, vt_all, kir_all, causal=False, top_k=top_k)
        a_attn = a_attn[:, :t].reshape(b * t, N_HEADS * HEAD_DIM)

    hist8 = jnp.pad(conv_hist, ((0, 0), (SUBLANES - (CONV_W - 1), 0), (0, 0)))
    y, conv_state = _out_ffn(xf, a_pool, a_attn, a_mem, lw['w_out'], lw['norm2'], lw['w_up'],
                             lw['conv_w'], lw['conv_b'], lw['w_down'], hist8, tm_ffn, t // tm_ffn)
    y = y.reshape(b, t, D_MODEL)
    k4 = k.reshape(b, t, N_KV_HEADS, HEAD_DIM)
    v4 = v.reshape(b, t, N_KV_HEADS, HEAD_DIM)
    ki3 = ki.reshape(b, t, IDX_DIM)
    u3 = u.reshape(b, t, POOL_WIDTH)
    pool_state = jnp.concatenate([pool_hist, u3], axis=1)[:, -POOL_HIST:]
    return y, k4, v4, ki3, pool_state, conv_state


def kernel(x_prompt, x_sample, mem_prompt, cache_k, cache_v, cache_kidx, cache_mem_k, cache_mem_v,
           state_pool, state_ffn_conv, norm1, w_in, q_norm, k_norm, pool_w, pool_scale, mem_norm,
           w_mem_k, w_mem_v, mem_q_norm, mem_k_norm, w_out, norm2, w_up, conv_w, conv_b, w_down):
    depth = norm1.shape[0]
    bp, sp, _ = x_prompt.shape
    bs, ts, _ = x_sample.shape
    p_len = cache_k.shape[2]
    blk = np.kron(np.eye(LANES // HEAD_DIM), np.ones((HEAD_DIM, HEAD_DIM))) / HEAD_DIM
    bmat = jnp.asarray(blk, BF16)
    xp, xs = x_prompt, x_sample
    p_states, s_states = [], []
    for l in range(depth):
        gw = POOL_WIDTH // POOL_GROUPS
        pw_bd = jnp.zeros((POOL_WIDTH, POOL_WIDTH), F32)
        for g in range(POOL_GROUPS):
            pw_bd = pw_bd.at[g * gw:(g + 1) * gw, g * gw:(g + 1) * gw].set(pool_w[l, g])
        lw = dict(norm1=norm1[l], wp=_prep_w_in(w_in[l]), q_norm=q_norm[l], k_norm=k_norm[l],
                  mem_q_norm=mem_q_norm[l], bmat=bmat, pw_bd=pw_bd.astype(BF16),
                  pool_scale=pool_scale[l], w_out=w_out[l], norm2=norm2[l], w_up=w_up[l],
                  conv_w=conv_w[l], conv_b=conv_b[l], w_down=w_down[l])

        mks, mvs = [], []
        for b in range(bp):
            mk_b, mv_b = _memory_kv(mem_prompt[b], mem_norm[l], w_mem_k[l], w_mem_v[l],
                                    mem_k_norm[l], bmat)
            mks.append(mk_b)
            mvs.append(mv_b)
        mk_p, mv_p = jnp.stack(mks), jnp.stack(mvs)

        xp, k_p, v_p, ki_p, pool_p, conv_p = _layer(
            xp, jnp.arange(sp), 0, None, mk_p, mv_p,
            jnp.zeros((bp, POOL_HIST, POOL_WIDTH), F32), jnp.zeros((bp, CONV_W - 1, 2 * D_FF), F32),
            lw, causal=True, tm_proj=512, tm_pool=512, tm_ffn=256)
        p_states.append((k_p, v_p, ki_p, mk_p.reshape(bp, MEM_TOKENS, MEM_HEADS, HEAD_DIM),
                         mv_p.reshape(bp, MEM_TOKENS, MEM_HEADS, HEAD_DIM), pool_p, conv_p))

        pos_s = jnp.tile(p_len + jnp.arange(ts), bs)
        xs, k_s, v_s, ki_s, pool_s, conv_s = _layer(
            xs, pos_s, p_len,
            (cache_k[l].reshape(bs, p_len, LANES), cache_v[l].reshape(bs, p_len, LANES), cache_kidx[l]),
            cache_mem_k[l].reshape(bs, MEM_TOKENS, MEM_WIDTH),
            cache_mem_v[l].reshape(bs, MEM_TOKENS, MEM_WIDTH),
            state_pool[l], state_ffn_conv[l], lw, causal=False, tm_proj=ts * bs, tm_pool=ts, tm_ffn=ts)
        s_states.append((k_s, v_s, ki_s, pool_s, conv_s))

    k_p, v_p, kidx_p, memk_p, memv_p, pool_p, conv_p = [jnp.stack(z) for z in zip(*p_states)]
    k_s, v_s, kidx_s, pool_s, conv_s = [jnp.stack(z) for z in zip(*s_states)]
    return (xp, xs, k_p, v_p, kidx_p, memk_p, memv_p, pool_p, conv_p, k_s, v_s, kidx_s, pool_s, conv_s)
```

```python
import functools
import math

import numpy as np
import jax
import jax.numpy as jnp
from jax import lax
from jax.experimental import pallas as pl
from jax.experimental.pallas import tpu as pltpu

F32, BF16, I32 = jnp.float32, jnp.bfloat16, jnp.int32

D_MODEL = 1024
CHUNK = 64
HEAD_DIM = 64
N_HEADS = 8
N_KV_HEADS = 2
IDX_HEADS = 8
IDX_DIM = 32
TOPK_MAX = 256
MEM_TOKENS = 256
MEM_HEADS = 4
MEM_WIDTH = MEM_HEADS * HEAD_DIM
POOL_WIDTH = 256
POOL_GROUPS = 4
POOL_WINDOWS = (2, 4, 8, 16)
POOL_HIST = 15
D_FF = 2816
CONV_W = 3
ROPE_THETA = 500000.0
ROT_DIM = HEAD_DIM // 4
IDX_ROT_DIM = IDX_DIM // 4
EPS = 1e-6
SPLITS = (POOL_WIDTH, N_HEADS * HEAD_DIM, N_KV_HEADS * HEAD_DIM, N_KV_HEADS * HEAD_DIM,
          IDX_HEADS * IDX_DIM, IDX_DIM, IDX_HEADS, MEM_WIDTH)

LANES = 128
SUBLANES = 8
VMEM_LIMIT = 56 * 1024 * 1024

COL_U = 0
COL_Q = 256
COL_K = 768
COL_V = 896
COL_QI = 1024
COL_KI = 1280
COL_WI = 1536
COL_QM = 1664
PROJ_W = 1920

QBLK = 128
KTILE = 512
KUNROLL = 2
LOG2E = 1.4426950408889634
Q_SCALE = HEAD_DIM ** -0.5 * LOG2E
FFN_CHUNK = 256
ALL_VISIBLE = -3.0e38
GEO_FLOOR = 1e-36
GEO_MIN_LARGE = 1e-30
MAX_SEARCH_STEPS = 96
MIN_SEARCH_STEPS = 12
VT_ROWS = 144
FAST_SHIFT_LIMIT = 60.0


def _dot(a, b):
    return jnp.dot(a, b, preferred_element_type=F32)


def _dot_nt(a, b):
    return lax.dot_general(a, b, (((1,), (1,)), ((), ())), preferred_element_type=F32)


def _full(shape):
    n = len(shape)
    return pl.BlockSpec(shape, lambda *_: (0,) * n)


def _resident(shape):
    n = len(shape)
    return pl.BlockSpec(shape, lambda *_: (0,) * n, pipeline_mode=pl.Buffered(1))


def _params(sem):
    return pltpu.CompilerParams(dimension_semantics=sem, vmem_limit_bytes=VMEM_LIMIT)


def _rms(x, g):
    ms = jnp.mean(x * x, axis=-1, keepdims=True)
    return x * lax.rsqrt(ms + EPS) * g


def _head_mean_sq(x, bmat):
    sq = x * x
    hi = sq.astype(BF16)
    lo = (sq - hi.astype(F32)).astype(BF16)
    return _dot(hi, bmat) + _dot(lo, bmat)


def _memkv_body(mem_ref, g_ref, wk_ref, wv_ref, gk_ref, bmat_ref, mk_ref, mv_ref):
    m = _rms(mem_ref[...], g_ref[...]).astype(BF16)
    kk = _dot(m, wk_ref[...])
    bmat = bmat_ref[...]
    for c in range(MEM_WIDTH // LANES):
        kc = kk[:, c * LANES:(c + 1) * LANES]
        ms = _head_mean_sq(kc, bmat)
        mk_ref[:, c * LANES:(c + 1) * LANES] = kc * lax.rsqrt(ms + EPS) * gk_ref[...]
    mv_ref[...] = _dot(m, wv_ref[...])


def _memory_kv(mem, mem_norm, w_mem_k, w_mem_v, mem_k_norm, bmat):
    m = mem.shape[0]
    return pl.pallas_call(
        _memkv_body,
        grid=(1,),
        in_specs=[_full((m, D_MODEL)), _full((1, D_MODEL)), _full((D_MODEL, MEM_WIDTH)),
                  _full((D_MODEL, MEM_WIDTH)), _full((1, LANES)), _full((LANES, LANES))],
        out_specs=[_full((m, MEM_WIDTH)), _full((m, MEM_WIDTH))],
        out_shape=[jax.ShapeDtypeStruct((m, MEM_WIDTH), F32)] * 2,
        compiler_params=_params(("arbitrary",)),
        name="memory_kv",
    )(mem, mem_norm.reshape(1, D_MODEL), w_mem_k.astype(BF16), w_mem_v.astype(BF16),
      jnp.tile(mem_k_norm, 2).reshape(1, LANES), bmat)


def _rope(x, c, sa, sb, shift):
    return (x * c + pltpu.roll(x, LANES - shift, axis=1) * sa + pltpu.roll(x, shift, axis=1) * sb)


def _proj_body(x_ref, cq_ref, saq_ref, sbq_ref, ci_ref, sai_ref, sbi_ref, g1_ref, w_ref,
               gq_ref, gk_ref, gm_ref, bmat_ref,
               u_ref, qx_ref, k_ref, kb_ref, v_ref, vt_ref, ki_ref, kir_ref, qi_ref, wi_ref, qm_ref):
    nb = _rms(x_ref[...], g1_ref[...]).astype(BF16)
    proj = _dot(nb, w_ref[...])
    u_ref[...] = proj[:, COL_U:COL_U + POOL_WIDTH]

    bmat = bmat_ref[...]
    cq, saq, sbq = cq_ref[...], saq_ref[...], sbq_ref[...]
    half_q = ROT_DIM // 2

    def head_norm(xc, g):
        ms = _head_mean_sq(xc, bmat)
        return xc * lax.rsqrt(ms + EPS) * g

    lane = lax.broadcasted_iota(I32, (1, LANES), 1)
    low = lane < HEAD_DIM
    n_pair = N_HEADS // 2
    for g in range(n_pair):
        qc = proj[:, COL_Q + g * LANES:COL_Q + (g + 1) * LANES]
        qc = _rope(head_norm(qc, gq_ref[...]), cq, saq, sbq, half_q) * Q_SCALE
        qx_ref[g] = jnp.where(low, qc, 0.0).astype(BF16)
        qx_ref[n_pair + g] = jnp.where(low, 0.0, qc).astype(BF16)

    kc = _rope(head_norm(proj[:, COL_K:COL_K + LANES], gk_ref[...]), cq, saq, sbq, half_q)
    k_ref[...] = kc
    kb_ref[...] = kc.astype(BF16)

    vc = proj[:, COL_V:COL_V + LANES]
    v_ref[...] = vc
    vt_ref[0:LANES, :] = vc.T.astype(BF16)
    extra = lax.broadcasted_iota(I32, (VT_ROWS - LANES, vc.shape[0]), 0)
    vt_ref[LANES:VT_ROWS, :] = jnp.where(extra == 0, 1.0, 0.0).astype(BF16)

    ci, sai, sbi = ci_ref[...], sai_ref[...], sbi_ref[...]
    half_i = IDX_ROT_DIM // 2
    for c in range(IDX_HEADS * IDX_DIM // LANES):
        sl = slice(c * LANES, (c + 1) * LANES)
        qic = _rope(proj[:, COL_QI + c * LANES:COL_QI + (c + 1) * LANES], ci, sai, sbi, half_i)
        qi_ref[:, sl] = (qic * (IDX_DIM ** -0.5)).astype(BF16)
        kic = _rope(proj[:, COL_KI + c * LANES:COL_KI + (c + 1) * LANES], ci, sai, sbi, half_i)
        kir_ref[:, sl] = kic.astype(BF16)
        if c == 0:
            ki_ref[...] = kic[:, :IDX_DIM]

    wi_ref[...] = proj[:, COL_WI:COL_WI + LANES] * (IDX_HEADS ** -0.5)

    for c in range(MEM_WIDTH // LANES):
        qmc = head_norm(proj[:, COL_QM + c * LANES:COL_QM + (c + 1) * LANES], gm_ref[...])
        qm_ref[:, c * LANES:(c + 1) * LANES] = (qmc * (HEAD_DIM ** -0.5)).astype(BF16)


def _rope_tables(pos, rot_dim, period):
    half = rot_dim // 2
    inv = ROPE_THETA ** (-jnp.arange(half, dtype=F32) / half)
    ang = pos.astype(F32)[:, None] * inv[None, :]
    cos, sin = jnp.cos(ang), jnp.sin(ang)
    lane = np.arange(LANES) % period
    first = lane < half
    second = (lane >= half) & (lane < rot_dim)
    idx = np.where(first, lane, np.where(second, lane - half, 0))
    c = jnp.where(jnp.asarray(first | second)[None, :], cos[:, idx], 1.0)
    sa = jnp.where(jnp.asarray(first)[None, :], -sin[:, idx], 0.0)
    sb = jnp.where(jnp.asarray(second)[None, :], sin[:, idx], 0.0)
    return c, sa, sb


def _prep_w_in(w_in):
    cuts = np.cumsum(SPLITS)[:-1]
    wu, wq, wk, wv, wqi, wki, wwi, wqm = jnp.split(w_in, [int(c) for c in cuts], axis=1)
    order = [h for g in range(N_HEADS // 2) for h in (g, N_HEADS // 2 + g)]
    wq = wq.reshape(D_MODEL, N_HEADS, HEAD_DIM)[:, order, :].reshape(D_MODEL, N_HEADS * HEAD_DIM)
    wki = jnp.tile(wki, (1, IDX_HEADS))
    wwi = jnp.pad(wwi, ((0, 0), (0, LANES - IDX_HEADS)))
    return jnp.concatenate([wu, wq, wk, wv, wqi, wki, wwi, wqm], axis=1).astype(BF16)


def _project(x, pos, norm1, wp, q_norm, k_norm, mem_q_norm, bmat, tm):
    t = x.shape[0]
    cq, saq, sbq = _rope_tables(pos, ROT_DIM, HEAD_DIM)
    ci, sai, sbi = _rope_tables(pos, IDX_ROT_DIM, IDX_DIM)
    row = lambda w: pl.BlockSpec((tm, w), lambda i: (i, 0))
    tab = row(LANES)
    out_shape = [
        jax.ShapeDtypeStruct((t, POOL_WIDTH), F32),
        jax.ShapeDtypeStruct((N_HEADS, t, LANES), BF16),
        jax.ShapeDtypeStruct((t, LANES), F32),
        jax.ShapeDtypeStruct((t, LANES), BF16),
        jax.ShapeDtypeStruct((t, LANES), F32),
        jax.ShapeDtypeStruct((VT_ROWS, t), BF16),
        jax.ShapeDtypeStruct((t, IDX_DIM), F32),
        jax.ShapeDtypeStruct((t, IDX_HEADS * IDX_DIM), BF16),
        jax.ShapeDtypeStruct((t, IDX_HEADS * IDX_DIM), BF16),
        jax.ShapeDtypeStruct((t, LANES), F32),
        jax.ShapeDtypeStruct((t, MEM_WIDTH), BF16),
    ]
    out_specs = [
        row(POOL_WIDTH),
        pl.BlockSpec((N_HEADS, tm, LANES), lambda i: (0, i, 0)),
        row(LANES), row(LANES), row(LANES),
        pl.BlockSpec((VT_ROWS, tm), lambda i: (0, i)),
        row(IDX_DIM), row(IDX_HEADS * IDX_DIM), row(IDX_HEADS * IDX_DIM), row(LANES), row(MEM_WIDTH),
    ]
    return pl.pallas_call(
        _proj_body,
        grid=(t // tm,),
        in_specs=[row(D_MODEL), tab, tab, tab, tab, tab, tab, _full((1, D_MODEL)),
                  _resident((D_MODEL, PROJ_W)), _full((1, LANES)), _full((1, LANES)),
                  _full((1, LANES)), _full((LANES, LANES))],
        out_specs=out_specs,
        out_shape=out_shape,
        compiler_params=_params(("arbitrary",)),
        name="input_projection",
    )(x, cq, saq, sbq, ci, sai, sbi, norm1.reshape(1, D_MODEL), wp,
      jnp.tile(q_norm, 2).reshape(1, LANES), jnp.tile(k_norm, 2).reshape(1, LANES),
      jnp.tile(mem_q_norm, 2).reshape(1, LANES), bmat)


def _pool_mem_body(u_ref, hist_ref, pw_ref, ps_ref, qm_ref, mk_ref, mv_ref,
                   apool_ref, amem_ref, carry_ref, *, tiles_per_seq, pos0):
    j = pl.program_id(0)
    tm = u_ref.shape[0]
    tile_in_seq = j % tiles_per_seq

    @pl.when(tile_in_seq == 0)
    def _():
        carry_ref[...] = hist_ref[0]

    u = u_ref[...]
    ext = jnp.concatenate([carry_ref[...], u], axis=0)
    carry_ref[...] = u[tm - 2 * SUBLANES:, :]
    s2 = ext + pltpu.roll(ext, 1, axis=0)
    s4 = s2 + pltpu.roll(s2, 2, axis=0)
    s8 = s4 + pltpu.roll(s4, 4, axis=0)
    s16 = s8 + pltpu.roll(s8, 8, axis=0)
    hs = 2 * SUBLANES
    gw = POOL_WIDTH // POOL_GROUPS
    lane = lax.broadcasted_iota(I32, (1, POOL_WIDTH), 1)
    win_sum = jnp.where(lane < gw, s2[hs:], jnp.where(lane < 2 * gw, s4[hs:],
                        jnp.where(lane < 3 * gw, s8[hs:], s16[hs:])))
    win = jnp.where(lane < gw, POOL_WINDOWS[0], jnp.where(lane < 2 * gw, POOL_WINDOWS[1],
                    jnp.where(lane < 3 * gw, POOL_WINDOWS[2], POOL_WINDOWS[3])))
    pos = pos0 + tile_in_seq * tm + lax.broadcasted_iota(I32, (tm, 1), 0)
    cnt = jnp.minimum(win, pos + 1).astype(F32)
    z = win_sum / cnt - u
    y = _dot(z.astype(BF16), pw_ref[...]) * ps_ref[...]
    apool_ref[...] = y.astype(BF16)

    qm = qm_ref[...]
    lane_m = lax.broadcasted_iota(I32, (1, MEM_WIDTH), 1)
    mk = mk_ref[0].astype(BF16)
    mv = mv_ref[0].astype(BF16)
    zero = jnp.zeros_like(qm)
    qs = jnp.concatenate(
        [jnp.where((lane_m >= h * HEAD_DIM) & (lane_m < (h + 1) * HEAD_DIM), qm, zero)
         for h in range(MEM_HEADS)], axis=0)
    logits = _dot_nt(qs, mk)
    mx = jnp.max(logits, axis=-1, keepdims=True)
    p = jnp.exp(logits - mx)
    den = jnp.sum(p, axis=-1, keepdims=True)
    o = _dot(p.astype(BF16), mv) / den
    out = jnp.zeros((tm, MEM_WIDTH), F32)
    for h in range(MEM_HEADS):
        oh = o[h * tm:(h + 1) * tm]
        out = jnp.where((lane_m >= h * HEAD_DIM) & (lane_m < (h + 1) * HEAD_DIM), oh, out)
    amem_ref[...] = out.astype(BF16)


def _pool_mem(u, hist, pw_bd, pool_scale, qm, mk, mv, tm, tiles_per_seq, pos0):
    t = u.shape[0]
    seq = lambda j: j // tiles_per_seq
    return pl.pallas_call(
        functools.partial(_pool_mem_body, tiles_per_seq=tiles_per_seq, pos0=pos0),
        grid=(t // tm,),
        in_specs=[pl.BlockSpec((tm, POOL_WIDTH), lambda j: (j, 0)),
                  pl.BlockSpec((1, 2 * SUBLANES, POOL_WIDTH), lambda j: (seq(j), 0, 0)),
                  _full((POOL_WIDTH, POOL_WIDTH)), _full((1, POOL_WIDTH)),
                  pl.BlockSpec((tm, MEM_WIDTH), lambda j: (j, 0)),
                  pl.BlockSpec((1, MEM_TOKENS, MEM_WIDTH), lambda j: (seq(j) % mk.shape[0], 0, 0)),
                  pl.BlockSpec((1, MEM_TOKENS, MEM_WIDTH), lambda j: (seq(j) % mk.shape[0], 0, 0))],
        out_specs=[pl.BlockSpec((tm, POOL_WIDTH), lambda j: (j, 0)),
                   pl.BlockSpec((tm, MEM_WIDTH), lambda j: (j, 0))],
        out_shape=[jax.ShapeDtypeStruct((t, POOL_WIDTH), BF16),
                   jax.ShapeDtypeStruct((t, MEM_WIDTH), BF16)],
        scratch_shapes=[pltpu.VMEM((2 * SUBLANES, POOL_WIDTH), F32)],
        compiler_params=_params(("arbitrary",)),
        name="pool_and_memory_attention",
    )(u, hist, pw_bd, pool_scale.reshape(1, POOL_WIDTH), qm, mk, mv)


def _midpoint(lo, hi):
    am = 0.5 * lo + 0.5 * hi
    alo, ahi = jnp.abs(lo), jnp.abs(hi)
    small, large = jnp.minimum(alo, ahi), jnp.maximum(alo, ahi)
    gm = jnp.sqrt(jnp.maximum(small, GEO_FLOOR) * large)
    gm = jnp.where(hi > 0.0, gm, -gm)
    use_gm = (large > 4.0 * small) & (large > GEO_MIN_LARGE)
    straddle = (lo < 0.0) & (hi > 0.0)
    return jnp.where(straddle, 0.0, jnp.where(use_gm, gm, am))


def _sublane_total(x):
    x = x + pltpu.roll(x, 4, axis=0)
    x = x + pltpu.roll(x, 2, axis=0)
    return x + pltpu.roll(x, 1, axis=0)


def _dsa_body(lim_ref, qx_ref, qi_ref, wi_ref, kb_ref, vt_ref, kir_ref, out_ref,
              score_ref, acc_ref, kmax_ref, *, causal, n_tiles_static, blocks_per_seq, top_k):
    qb = pl.program_id(0) % blocks_per_seq
    tk = KTILE
    if causal:
        n_tiles = ((qb + 1) * QBLK + tk - 1) // tk
    else:
        n_tiles = n_tiles_static
    limit = lim_ref[0]

    @pl.when(qb == 0)
    def _():
        ones = jnp.ones((LANES, LANES), BF16)

        def knorm(i, mx):
            kt = kb_ref[pl.ds(pl.multiple_of(i * tk, tk), tk), :].astype(F32)
            return jnp.maximum(mx, jnp.max(_dot((kt * kt).astype(BF16), ones), axis=0, keepdims=True))

        kmax_ref[...] = lax.fori_loop(0, n_tiles_static, knorm, jnp.zeros((1, LANES), F32))

    qi = qi_ref[...]
    lane_i = lax.broadcasted_iota(I32, (1, IDX_HEADS * IDX_DIM), 1)
    zero_qi = jnp.zeros_like(qi)
    qi_rows = jnp.concatenate(
        [jnp.where((lane_i >= h * IDX_DIM) & (lane_i < (h + 1) * IDX_DIM), qi, zero_qi)
         for h in range(IDX_HEADS)], axis=0)
    w_t = wi_ref[...].T
    n_cls = TOPK_MAX

    def score_tile(r0, cls_max):
        s = _dot_nt(kir_ref[pl.ds(r0, tk), :], qi_rows)
        score = jnp.zeros((tk, QBLK), F32)
        for h in range(IDX_HEADS):
            score = score + jnp.maximum(s[:, h * QBLK:(h + 1) * QBLK], 0.0) * w_t[h:h + 1, :]
        kidx = r0 + lax.broadcasted_iota(I32, (tk, 1), 0)
        score = jnp.where(kidx < limit, score, -jnp.inf)
        score_ref[pl.ds(r0, tk), :] = score
        for c in range(tk // n_cls):
            cls_max = jnp.maximum(cls_max, score[c * n_cls:(c + 1) * n_cls])
        return cls_max

    def score_step(i, cls_max):
        for sub in range(KUNROLL):
            cls_max = score_tile(pl.multiple_of((i * KUNROLL + sub) * tk, tk), cls_max)
        return cls_max

    n_steps = (n_tiles + KUNROLL - 1) // KUNROLL
    cls_max = lax.fori_loop(0, n_steps, score_step, jnp.full((n_cls, QBLK), -jnp.inf, F32))
    lo0 = jnp.min(cls_max, axis=0, keepdims=True)
    top = jnp.max(cls_max, axis=0, keepdims=True)
    hi0 = top + (jnp.abs(top) * 2.0 ** -20 + 1e-30)
    few = limit <= top_k
    rep = lambda x: jnp.broadcast_to(x, (SUBLANES, QBLK))
    lo0 = rep(jnp.where(few, ALL_VISIBLE, jnp.maximum(lo0, ALL_VISIBLE)))
    hi0 = rep(hi0)

    def count_ge(thr):
        def body(i, acc):
            r0 = pl.multiple_of(i * tk, tk)
            sc = score_ref[pl.ds(r0, tk), :].reshape(tk // SUBLANES, SUBLANES, QBLK)
            return acc + jnp.sum((sc >= thr[None]).astype(I32), axis=0)
        acc = lax.fori_loop(0, n_tiles, body, jnp.zeros((SUBLANES, QBLK), I32))
        return _sublane_total(acc)

    def probe(lo, hi, c_lo):
        mid = _midpoint(lo, hi)
        return mid, (c_lo != top_k) & (mid > lo) & (mid < hi)

    def search_step(st):
        lo, hi, c_lo, c_hi = st
        mid, open_ = probe(lo, hi, c_lo)
        c = count_ge(mid)
        up = open_ & (c >= top_k)
        dn = open_ & (c < top_k)
        return (jnp.where(up, mid, lo), jnp.where(dn, mid, hi),
                jnp.where(up, c, c_lo), jnp.where(dn, c, c_hi))

    def search_cond(st):
        step, lo, hi, c_lo, _ = st
        any_open = jnp.max(probe(lo, hi, c_lo)[1][0:1, :].astype(F32)) > 0.0
        return (step < MAX_SEARCH_STEPS) & any_open

    big = jnp.full((SUBLANES, QBLK), 2 * top_k, I32)
    c_lo0 = jnp.where(rep(few), top_k, big)
    st = lax.fori_loop(0, MIN_SEARCH_STEPS, lambda _, st: search_step(st),
                       (lo0, hi0, c_lo0, jnp.zeros_like(big)))
    _, thr, _, c_lo, c_hi = lax.while_loop(
        search_cond, lambda st: (st[0] + 1,) + search_step(st[1:]), (jnp.int32(MIN_SEARCH_STEPS),) + st)

    tied = c_lo != top_k
    need = jnp.where(tied, top_k - c_hi, jnp.int32(2 ** 30))

    @pl.when(jnp.max(tied.astype(I32)) > 0)
    def _():
        sub = lax.broadcasted_iota(I32, (SUBLANES, QBLK), 0)

        def fix(v, seen):
            r0 = pl.multiple_of(v * SUBLANES, SUBLANES)
            sc = score_ref[pl.ds(r0, SUBLANES), :]
            eq = (sc == thr).astype(I32)
            pre = eq
            for sft in (1, 2, 4):
                pre = pre + jnp.where(sub >= sft, pltpu.roll(pre, sft, axis=0), 0)
            rank = seen + pre - eq
            score_ref[pl.ds(r0, SUBLANES), :] = jnp.where((eq > 0) & (rank >= need), -jnp.inf, sc)
            return seen + _sublane_total(eq)

        lax.fori_loop(0, n_tiles * (tk // SUBLANES), fix, jnp.zeros((SUBLANES, QBLK), I32))

    qx = qx_ref[...].reshape(N_HEADS * QBLK, LANES)
    thr_row = thr[0:1, :]
    acc_ref[...] = jnp.zeros_like(acc_ref)

    qf = qx.astype(F32)
    qn2 = _dot_nt(jnp.ones((SUBLANES, LANES), BF16), (qf * qf).astype(BF16))[0:1, :]
    kmax2 = jnp.concatenate([kmax_ref[...]] * N_HEADS, axis=1)
    bound = jnp.sqrt(qn2 * kmax2) * 1.01
    fast = jnp.max(bound) <= FAST_SHIFT_LIMIT

    def finish(o):
        hpk = N_HEADS // N_KV_HEADS
        for g in range(N_HEADS // 2):
            parts = []
            for h in (2 * g, 2 * g + 1):
                jkv = h // hpk
                parts.append(o[jkv * HEAD_DIM:(jkv + 1) * HEAD_DIM, h * QBLK:(h + 1) * QBLK])
            blk = jnp.concatenate(parts, axis=0)
            out_ref[:, g * LANES:(g + 1) * LANES] = blk.T.astype(BF16)

    @pl.when(fast)
    def _():
        def attend_tile(r0):
            logit = _dot_nt(kb_ref[pl.ds(r0, tk), :], qx)
            sel = score_ref[pl.ds(r0, tk), :] >= thr_row
            ps = []
            for h in range(N_HEADS):
                sl = slice(h * QBLK, (h + 1) * QBLK)
                ps.append(jnp.where(sel, jnp.exp2(logit[:, sl] - bound[:, sl]), 0.0).astype(BF16))
            return _dot(vt_ref[:, pl.ds(r0, tk)], jnp.concatenate(ps, axis=1))

        def attend(i, carry):
            pv = attend_tile(pl.multiple_of(i * KUNROLL * tk, tk))
            for sub in range(1, KUNROLL):
                pv = pv + attend_tile(pl.multiple_of((i * KUNROLL + sub) * tk, tk))
            acc_ref[...] += pv
            return carry

        lax.fori_loop(0, n_steps, attend, 0)
        acc = acc_ref[...]
        finish(acc[:LANES] / acc[LANES:LANES + 1])

    @pl.when(jnp.logical_not(fast))
    def _():
        def attend(i, m):
            r0 = pl.multiple_of(i * tk, tk)
            logit = _dot_nt(kb_ref[pl.ds(r0, tk), :], qx)
            sel = score_ref[pl.ds(r0, tk), :] >= thr_row
            m_new = jnp.maximum(m, jnp.max(logit, axis=0, keepdims=True))
            alpha = jnp.exp2(m - m_new)
            ps = []
            for h in range(N_HEADS):
                sl = slice(h * QBLK, (h + 1) * QBLK)
                ps.append(jnp.where(sel, jnp.exp2(logit[:, sl] - m_new[:, sl]), 0.0).astype(BF16))
            p = jnp.concatenate(ps, axis=1)
            acc_ref[...] = acc_ref[...] * alpha + _dot(vt_ref[:, pl.ds(r0, tk)], p)
            return m_new

        lax.fori_loop(0, n_tiles, attend, jnp.full((1, N_HEADS * QBLK), -1e30, F32))
        acc = acc_ref[...]
        finish(acc[:LANES] / acc[LANES:LANES + 1])


def _dsa(limits, qx, qi, wi, kb, vt, kir, *, causal, top_k):
    nb, _, t, _ = qx.shape
    lp = kb.shape[1]
    nq = t // QBLK
    body = functools.partial(_dsa_body, causal=causal, n_tiles_static=lp // KTILE,
                             blocks_per_seq=nq, top_k=top_k)
    bq = lambda i: (i // nq, i % nq)
    return pl.pallas_call(
        body,
        grid=(nb * nq,),
        in_specs=[
            pl.BlockSpec((1, 1, QBLK), lambda i: (i, 0, 0)),
            pl.BlockSpec((None, N_HEADS, QBLK, LANES), lambda i: (bq(i)[0], 0, bq(i)[1], 0)),
            pl.BlockSpec((None, QBLK, IDX_HEADS * IDX_DIM), lambda i: (bq(i)[0], bq(i)[1], 0)),
            pl.BlockSpec((None, QBLK, LANES), lambda i: (bq(i)[0], bq(i)[1], 0)),
            pl.BlockSpec((None, lp, LANES), lambda i: (bq(i)[0], 0, 0), pipeline_mode=pl.Buffered(1)),
            pl.BlockSpec((None, VT_ROWS, lp), lambda i: (bq(i)[0], 0, 0), pipeline_mode=pl.Buffered(1)),
            pl.BlockSpec((None, lp, IDX_HEADS * IDX_DIM), lambda i: (bq(i)[0], 0, 0),
                         pipeline_mode=pl.Buffered(1)),
        ],
        out_specs=pl.BlockSpec((None, QBLK, N_HEADS * HEAD_DIM), lambda i: (bq(i)[0], bq(i)[1], 0)),
        out_shape=jax.ShapeDtypeStruct((nb, t, N_HEADS * HEAD_DIM), BF16),
        scratch_shapes=[pltpu.VMEM((lp, QBLK), F32), pltpu.VMEM((VT_ROWS, N_HEADS * QBLK), F32),
                        pltpu.VMEM((1, LANES), F32)],
        compiler_params=_params(("arbitrary",)),
        name="dsa_attention",
    )(limits, qx, qi, wi, kb, vt, kir)


def _ffn_body(x_ref, ap_ref, aa_ref, am_ref, wo_ref, g2_ref, wup_ref, cw_ref, cb_ref, wdn_ref,
              hist_ref, y_ref, cst_ref, carry_ref, acc_ref, *, tiles_per_seq):
    j = pl.program_id(0)
    tm = x_ref.shape[0]
    tile_in_seq = j % tiles_per_seq

    @pl.when(tile_in_seq == 0)
    def _():
        carry_ref[...] = hist_ref[0]

    mix = jnp.concatenate([ap_ref[...], aa_ref[...], am_ref[...]], axis=1)
    h = x_ref[...] + _dot(mix, wo_ref[...])
    nb = _rms(h, g2_ref[...]).astype(BF16)
    acc_ref[...] = h
    n_chunk = D_FF // FFN_CHUNK
    for c in range(n_chunk):
        conv = []
        for part in range(2):
            c0 = part * D_FF + c * FFN_CHUNK
            cols = slice(c0, c0 + FFN_CHUNK)
            up = _dot(nb, wup_ref[:, cols])
            ext = jnp.concatenate([carry_ref[:, cols], up], axis=0)
            carry_ref[:, cols] = up[tm - SUBLANES:, :]
            cst_ref[0, :, cols] = up[tm - (CONV_W - 1):, :]

            cv = (cb_ref[:, cols] + ext[SUBLANES:] * cw_ref[2:3, cols]
                  + pltpu.roll(ext, 1, axis=0)[SUBLANES:] * cw_ref[1:2, cols]
                  + pltpu.roll(ext, 2, axis=0)[SUBLANES:] * cw_ref[0:1, cols])
            conv.append(cv)
        gate, val = conv
        act = gate / (1.0 + jnp.exp(-gate)) * val
        acc_ref[...] += _dot(act.astype(BF16), wdn_ref[c * FFN_CHUNK:(c + 1) * FFN_CHUNK, :])
    y_ref[...] = acc_ref[...]


def _out_ffn(x, a_pool, a_attn, a_mem, w_out, norm2, w_up, conv_w, conv_b, w_down, hist, tm,
             tiles_per_seq):
    t = x.shape[0]
    n_seq = t // (tm * tiles_per_seq)
    row = lambda w: pl.BlockSpec((tm, w), lambda j: (j, 0))
    seq = lambda j: j // tiles_per_seq
    return pl.pallas_call(
        functools.partial(_ffn_body, tiles_per_seq=tiles_per_seq),
        grid=(t // tm,),
        in_specs=[row(D_MODEL), row(POOL_WIDTH), row(N_HEADS * HEAD_DIM), row(MEM_WIDTH),
                  _resident((D_MODEL, D_MODEL)), _full((1, D_MODEL)),
                  _resident((D_MODEL, 2 * D_FF)), _full((CONV_W, 2 * D_FF)), _full((1, 2 * D_FF)),
                  _resident((D_FF, D_MODEL)),
                  pl.BlockSpec((1, SUBLANES, 2 * D_FF), lambda j: (seq(j), 0, 0))],
        out_specs=[row(D_MODEL),
                   pl.BlockSpec((1, CONV_W - 1, 2 * D_FF), lambda j: (seq(j), 0, 0))],
        out_shape=[jax.ShapeDtypeStruct((t, D_MODEL), F32),
                   jax.ShapeDtypeStruct((n_seq, CONV_W - 1, 2 * D_FF), F32)],
        scratch_shapes=[pltpu.VMEM((SUBLANES, 2 * D_FF), F32), pltpu.VMEM((tm, D_MODEL), F32)],
        compiler_params=_params(("arbitrary",)),
        name="out_proj_conv_ffn",
    )(x, a_pool, a_attn, a_mem, w_out.astype(BF16), norm2.reshape(1, D_MODEL), w_up.astype(BF16),
      conv_w, conv_b.reshape(1, 2 * D_FF), w_down.astype(BF16), hist)


def _pad_rows(a, rows, axis):
    pad = [(0, 0)] * a.ndim
    pad[axis] = (0, rows - a.shape[axis])
    return jnp.pad(a, pad)


def _layer(x, pos, pos0, keys_past, mk, mv, pool_hist, conv_hist, lw, *, causal, tm_proj, tm_pool,
           tm_ffn):
    b, t, _ = x.shape
    xf = x.reshape(b * t, D_MODEL)
    (u, qx, k, kb, v, vt, ki, kir, qi, wi, qm) = _project(
        xf, pos, lw['norm1'], lw['wp'], lw['q_norm'], lw['k_norm'], lw['mem_q_norm'], lw['bmat'],
        tm_proj)

    hist16 = jnp.pad(pool_hist, ((0, 0), (2 * SUBLANES - POOL_HIST, 0), (0, 0)))
    a_pool, a_mem = _pool_mem(u, hist16, lw['pw_bd'], lw['pool_scale'], qm, mk, mv, tm_pool,
                              t // tm_pool, pos0)

    if keys_past is None:
        l_keys = t
        top_k = min(TOPK_MAX, l_keys // 4)
        lp = -(-l_keys // (KTILE * KUNROLL)) * (KTILE * KUNROLL)
        kb_all = _pad_rows(kb.reshape(b, t, LANES), lp, 1)
        vt_all = _pad_rows(vt.reshape(VT_ROWS, b, t).transpose(1, 0, 2), lp, 2)
        kir_all = _pad_rows(kir.reshape(b, t, -1), lp, 1)
        limits = ((jnp.arange(t, dtype=I32) // CHUNK + 1) * CHUNK)
        limits = jnp.tile(limits.reshape(1, t // QBLK, 1, QBLK), (b, 1, 1, 1)).reshape(-1, 1, QBLK)
        qx_b = qx.reshape(N_HEADS, b, t, LANES).transpose(1, 0, 2, 3)
        qi_b = qi.reshape(b, t, -1)
        wi_b = wi.reshape(b, t, LANES)
        a_attn = _dsa(limits, qx_b, qi_b, wi_b, kb_all, vt_all, kir_all, causal=True, top_k=top_k)
        a_attn = a_attn.reshape(b * t, N_HEADS * HEAD_DIM)
    else:
        ck, cv, cki = keys_past
        p_len = ck.shape[1]
        l_keys = p_len + t
        top_k = min(TOPK_MAX, l_keys // 4)
        lp = -(-l_keys // (KTILE * KUNROLL)) * (KTILE * KUNROLL)
        kb_all = _pad_rows(jnp.concatenate([ck.astype(BF16), kb.reshape(b, t, LANES)], axis=1), lp, 1)
        v_new_t = vt.reshape(VT_ROWS, b, t).transpose(1, 0, 2)
        ones_rows = jnp.zeros((b, VT_ROWS - LANES, p_len), BF16).at[:, 0, :].set(1.0)
        cv_t = jnp.concatenate([cv.astype(BF16).transpose(0, 2, 1), ones_rows], axis=1)
        vt_all = _pad_rows(jnp.concatenate([cv_t, v_new_t], axis=2), lp, 2)
        kir_all = _pad_rows(jnp.concatenate(
            [jnp.tile(cki.astype(BF16), (1, 1, IDX_HEADS)), kir.reshape(b, t, -1)], axis=1), lp, 1)
        reps = QBLK // t
        limits = jnp.full((b, 1, QBLK), l_keys, I32)
        qx_b = jnp.tile(qx.reshape(N_HEADS, b, t, LANES).transpose(1, 0, 2, 3), (1, 1, reps, 1))
        qi_b = jnp.tile(qi.reshape(b, t, -1), (1, reps, 1))
        wi_b = jnp.tile(wi.reshape(b, t, LANES), (1, reps, 1))
        a_attn = _dsa(limits, qx_b, qi_b, wi_b, kb_all, vt_all, kir_all, causal=False, top_k=top_k)
        a_attn = a_attn[:, :t].reshape(b * t, N_HEADS * HEAD_DIM)

    hist8 = jnp.pad(conv_hist, ((0, 0), (SUBLANES - (CONV_W - 1), 0), (0, 0)))
    y, conv_state = _out_ffn(xf, a_pool, a_attn, a_mem, lw['w_out'], lw['norm2'], lw['w_up'],
                             lw['conv_w'], lw['conv_b'], lw['w_down'], hist8, tm_ffn, t // tm_ffn)
    y = y.reshape(b, t, D_MODEL)
    k4 = k.reshape(b, t, N_KV_HEADS, HEAD_DIM)
    v4 = v.reshape(b, t, N_KV_HEADS, HEAD_DIM)
    ki3 = ki.reshape(b, t, IDX_DIM)
    u3 = u.reshape(b, t, POOL_WIDTH)
    pool_state = jnp.concatenate([pool_hist, u3], axis=1)[:, -POOL_HIST:]
    return y, k4, v4, ki3, pool_state, conv_state


def kernel(x_prompt, x_sample, mem_prompt, cache_k, cache_v, cache_kidx, cache_mem_k, cache_mem_v,
           state_pool, state_ffn_conv, norm1, w_in, q_norm, k_norm, pool_w, pool_scale, mem_norm,
           w_mem_k, w_mem_v, mem_q_norm, mem_k_norm, w_out, norm2, w_up, conv_w, conv_b, w_down):
    depth = norm1.shape[0]
    bp, sp, _ = x_prompt.shape
    bs, ts, _ = x_sample.shape
    p_len = cache_k.shape[2]
    blk = np.kron(np.eye(LANES // HEAD_DIM), np.ones((HEAD_DIM, HEAD_DIM))) / HEAD_DIM
    bmat = jnp.asarray(blk, BF16)
    xp, xs = x_prompt, x_sample
    p_states, s_states = [], []
    for l in range(depth):
        gw = POOL_WIDTH // POOL_GROUPS
        pw_bd = jnp.zeros((POOL_WIDTH, POOL_WIDTH), F32)
        for g in range(POOL_GROUPS):
            pw_bd = pw_bd.at[g * gw:(g + 1) * gw, g * gw:(g + 1) * gw].set(pool_w[l, g])
        lw = dict(norm1=norm1[l], wp=_prep_w_in(w_in[l]), q_norm=q_norm[l], k_norm=k_norm[l],
                  mem_q_norm=mem_q_norm[l], bmat=bmat, pw_bd=pw_bd.astype(BF16),
                  pool_scale=pool_scale[l], w_out=w_out[l], norm2=norm2[l], w_up=w_up[l],
                  conv_w=conv_w[l], conv_b=conv_b[l], w_down=w_down[l])

        mks, mvs = [], []
        for b in range(bp):
            mk_b, mv_b = _memory_kv(mem_prompt[b], mem_norm[l], w_mem_k[l], w_mem_v[l],
                                    mem_k_norm[l], bmat)
            mks.append(mk_b)
            mvs.append(mv_b)
        mk_p, mv_p = jnp.stack(mks), jnp.stack(mvs)

        xp, k_p, v_p, ki_p, pool_p, conv_p = _layer(
            xp, jnp.arange(sp), 0, None, mk_p, mv_p,
            jnp.zeros((bp, POOL_HIST, POOL_WIDTH), F32), jnp.zeros((bp, CONV_W - 1, 2 * D_FF), F32),
            lw, causal=True, tm_proj=512, tm_pool=512, tm_ffn=512)
        p_states.append((k_p, v_p, ki_p, mk_p.reshape(bp, MEM_TOKENS, MEM_HEADS, HEAD_DIM),
                         mv_p.reshape(bp, MEM_TOKENS, MEM_HEADS, HEAD_DIM), pool_p, conv_p))

        pos_s = jnp.tile(p_len + jnp.arange(ts), bs)
        xs, k_s, v_s, ki_s, pool_s, conv_s = _layer(
            xs, pos_s, p_len,
            (cache_k[l].reshape(bs, p_len, LANES), cache_v[l].reshape(bs, p_len, LANES), cache_kidx[l]),
            cache_mem_k[l].reshape(bs, MEM_TOKENS, MEM_WIDTH),
            cache_mem_v[l].reshape(bs, MEM_TOKENS, MEM_WIDTH),
            state_pool[l], state_ffn_conv[l], lw, causal=False, tm_proj=ts * bs, tm_pool=ts, tm_ffn=ts)
        s_states.append((k_s, v_s, ki_s, pool_s, conv_s))

    k_p, v_p, kidx_p, memk_p, memv_p, pool_p, conv_p = [jnp.stack(z) for z in zip(*p_states)]
    k_s, v_s, kidx_s, pool_s, conv_s = [jnp.stack(z) for z in zip(*s_states)]
    return (xp, xs, k_p, v_p, kidx_p, memk_p, memv_p, pool_p, conv_p, k_s, v_s, kidx_s, pool_s, conv_s)
```

```python
import functools
import math

import numpy as np
import jax
import jax.numpy as jnp
from jax import lax
from jax.experimental import pallas as pl
from jax.experimental.pallas import tpu as pltpu

F32, BF16, I32 = jnp.float32, jnp.bfloat16, jnp.int32

D_MODEL = 1024
CHUNK = 64
HEAD_DIM = 64
N_HEADS = 8
N_KV_HEADS = 2
IDX_HEADS = 8
IDX_DIM = 32
TOPK_MAX = 256
MEM_TOKENS = 256
MEM_HEADS = 4
MEM_WIDTH = MEM_HEADS * HEAD_DIM
POOL_WIDTH = 256
POOL_GROUPS = 4
POOL_WINDOWS = (2, 4, 8, 16)
POOL_HIST = 15
D_FF = 2816
CONV_W = 3
ROPE_THETA = 500000.0
ROT_DIM = HEAD_DIM // 4
IDX_ROT_DIM = IDX_DIM // 4
EPS = 1e-6
SPLITS = (POOL_WIDTH, N_HEADS * HEAD_DIM, N_KV_HEADS * HEAD_DIM, N_KV_HEADS * HEAD_DIM,
          IDX_HEADS * IDX_DIM, IDX_DIM, IDX_HEADS, MEM_WIDTH)

LANES = 128
SUBLANES = 8
VMEM_LIMIT = 56 * 1024 * 1024

COL_U = 0
COL_Q = 256
COL_K = 768
COL_V = 896
COL_QI = 1024
COL_KI = 1280
COL_WI = 1536
COL_QM = 1664
PROJ_W = 1920

QBLK = 128
KTILE = 512
KUNROLL = 2
COUNT_CHAINS = 8
LOG2E = 1.4426950408889634
Q_SCALE = HEAD_DIM ** -0.5 * LOG2E
FFN_CHUNK = 256
ALL_VISIBLE = -3.0e38
TINY = 1.1754944e-38
SCALE_JUMP = 2.0 ** -6
MAX_SEARCH_STEPS = 96
MIN_SEARCH_STEPS = 14
VT_ROWS = 144
FAST_SHIFT_LIMIT = 60.0


def _dot(a, b):
    return jnp.dot(a, b, preferred_element_type=F32)


def _dot_nt(a, b):
    return lax.dot_general(a, b, (((1,), (1,)), ((), ())), preferred_element_type=F32)


def _full(shape):
    n = len(shape)
    return pl.BlockSpec(shape, lambda *_: (0,) * n)


def _resident(shape):
    n = len(shape)
    return pl.BlockSpec(shape, lambda *_: (0,) * n, pipeline_mode=pl.Buffered(1))


def _params(sem):
    return pltpu.CompilerParams(dimension_semantics=sem, vmem_limit_bytes=VMEM_LIMIT)


def _rms(x, g):
    ms = jnp.mean(x * x, axis=-1, keepdims=True)
    return x * lax.rsqrt(ms + EPS) * g


def _head_mean_sq(x, bmat):
    sq = x * x
    hi = sq.astype(BF16)
    lo = (sq - hi.astype(F32)).astype(BF16)
    return _dot(hi, bmat) + _dot(lo, bmat)


def _memkv_body(mem_ref, g_ref, wk_ref, wv_ref, gk_ref, bmat_ref, mk_ref, mv_ref):
    m = _rms(mem_ref[...], g_ref[...]).astype(BF16)
    kk = _dot(m, wk_ref[...])
    bmat = bmat_ref[...]
    for c in range(MEM_WIDTH // LANES):
        kc = kk[:, c * LANES:(c + 1) * LANES]
        ms = _head_mean_sq(kc, bmat)
        mk_ref[:, c * LANES:(c + 1) * LANES] = kc * lax.rsqrt(ms + EPS) * gk_ref[...]
    mv_ref[...] = _dot(m, wv_ref[...])


def _memory_kv(mem, mem_norm, w_mem_k, w_mem_v, mem_k_norm, bmat):
    m = mem.shape[0]
    return pl.pallas_call(
        _memkv_body,
        grid=(1,),
        in_specs=[_full((m, D_MODEL)), _full((1, D_MODEL)), _full((D_MODEL, MEM_WIDTH)),
                  _full((D_MODEL, MEM_WIDTH)), _full((1, LANES)), _full((LANES, LANES))],
        out_specs=[_full((m, MEM_WIDTH)), _full((m, MEM_WIDTH))],
        out_shape=[jax.ShapeDtypeStruct((m, MEM_WIDTH), F32)] * 2,
        compiler_params=_params(("arbitrary",)),
        name="memory_kv",
    )(mem, mem_norm.reshape(1, D_MODEL), w_mem_k.astype(BF16), w_mem_v.astype(BF16),
      jnp.tile(mem_k_norm, 2).reshape(1, LANES), bmat)


def _rope(x, c, sa, sb, shift):
    return (x * c + pltpu.roll(x, LANES - shift, axis=1) * sa + pltpu.roll(x, shift, axis=1) * sb)


def _proj_body(x_ref, cq_ref, saq_ref, sbq_ref, ci_ref, sai_ref, sbi_ref, g1_ref, w_ref,
               gq_ref, gk_ref, gm_ref, bmat_ref,
               u_ref, qx_ref, k_ref, kb_ref, v_ref, vt_ref, ki_ref, kir_ref, qi_ref, wi_ref, qm_ref):
    nb = _rms(x_ref[...], g1_ref[...]).astype(BF16)
    proj = _dot(nb, w_ref[...])
    u_ref[...] = proj[:, COL_U:COL_U + POOL_WIDTH]

    bmat = bmat_ref[...]
    cq, saq, sbq = cq_ref[...], saq_ref[...], sbq_ref[...]
    half_q = ROT_DIM // 2

    def head_norm(xc, g):
        ms = _head_mean_sq(xc, bmat)
        return xc * lax.rsqrt(ms + EPS) * g

    lane = lax.broadcasted_iota(I32, (1, LANES), 1)
    low = lane < HEAD_DIM
    n_pair = N_HEADS // 2
    for g in range(n_pair):
        qc = proj[:, COL_Q + g * LANES:COL_Q + (g + 1) * LANES]
        qc = _rope(head_norm(qc, gq_ref[...]), cq, saq, sbq, half_q) * Q_SCALE
        qx_ref[g] = jnp.where(low, qc, 0.0).astype(BF16)
        qx_ref[n_pair + g] = jnp.where(low, 0.0, qc).astype(BF16)

    kc = _rope(head_norm(proj[:, COL_K:COL_K + LANES], gk_ref[...]), cq, saq, sbq, half_q)
    k_ref[...] = kc
    kb_ref[...] = kc.astype(BF16)

    vc = proj[:, COL_V:COL_V + LANES]
    v_ref[...] = vc
    vt_ref[0:LANES, :] = vc.T.astype(BF16)
    extra = lax.broadcasted_iota(I32, (VT_ROWS - LANES, vc.shape[0]), 0)
    vt_ref[LANES:VT_ROWS, :] = jnp.where(extra == 0, 1.0, 0.0).astype(BF16)

    ci, sai, sbi = ci_ref[...], sai_ref[...], sbi_ref[...]
    half_i = IDX_ROT_DIM // 2
    for c in range(IDX_HEADS * IDX_DIM // LANES):
        sl = slice(c * LANES, (c + 1) * LANES)
        qic = _rope(proj[:, COL_QI + c * LANES:COL_QI + (c + 1) * LANES], ci, sai, sbi, half_i)
        qi_ref[:, sl] = (qic * (IDX_DIM ** -0.5)).astype(BF16)
        kic = _rope(proj[:, COL_KI + c * LANES:COL_KI + (c + 1) * LANES], ci, sai, sbi, half_i)
        kir_ref[:, sl] = kic.astype(BF16)
        if c == 0:
            ki_ref[...] = kic[:, :IDX_DIM]

    wi_ref[...] = proj[:, COL_WI:COL_WI + LANES] * (IDX_HEADS ** -0.5)

    for c in range(MEM_WIDTH // LANES):
        qmc = head_norm(proj[:, COL_QM + c * LANES:COL_QM + (c + 1) * LANES], gm_ref[...])
        qm_ref[:, c * LANES:(c + 1) * LANES] = (qmc * (HEAD_DIM ** -0.5)).astype(BF16)


def _rope_tables(pos, rot_dim, period):
    half = rot_dim // 2
    inv = ROPE_THETA ** (-jnp.arange(half, dtype=F32) / half)
    ang = pos.astype(F32)[:, None] * inv[None, :]
    cos, sin = jnp.cos(ang), jnp.sin(ang)
    lane = np.arange(LANES) % period
    first = lane < half
    second = (lane >= half) & (lane < rot_dim)
    idx = np.where(first, lane, np.where(second, lane - half, 0))
    c = jnp.where(jnp.asarray(first | second)[None, :], cos[:, idx], 1.0)
    sa = jnp.where(jnp.asarray(first)[None, :], -sin[:, idx], 0.0)
    sb = jnp.where(jnp.asarray(second)[None, :], sin[:, idx], 0.0)
    return c, sa, sb


def _prep_w_in(w_in):
    cuts = np.cumsum(SPLITS)[:-1]
    wu, wq, wk, wv, wqi, wki, wwi, wqm = jnp.split(w_in, [int(c) for c in cuts], axis=1)
    order = [h for g in range(N_HEADS // 2) for h in (g, N_HEADS // 2 + g)]
    wq = wq.reshape(D_MODEL, N_HEADS, HEAD_DIM)[:, order, :].reshape(D_MODEL, N_HEADS * HEAD_DIM)
    wki = jnp.tile(wki, (1, IDX_HEADS))
    wwi = jnp.pad(wwi, ((0, 0), (0, LANES - IDX_HEADS)))
    return jnp.concatenate([wu, wq, wk, wv, wqi, wki, wwi, wqm], axis=1).astype(BF16)


def _project(x, pos, norm1, wp, q_norm, k_norm, mem_q_norm, bmat, tm):
    t = x.shape[0]
    cq, saq, sbq = _rope_tables(pos, ROT_DIM, HEAD_DIM)
    ci, sai, sbi = _rope_tables(pos, IDX_ROT_DIM, IDX_DIM)
    row = lambda w: pl.BlockSpec((tm, w), lambda i: (i, 0))
    tab = row(LANES)
    out_shape = [
        jax.ShapeDtypeStruct((t, POOL_WIDTH), F32),
        jax.ShapeDtypeStruct((N_HEADS, t, LANES), BF16),
        jax.ShapeDtypeStruct((t, LANES), F32),
        jax.ShapeDtypeStruct((t, LANES), BF16),
        jax.ShapeDtypeStruct((t, LANES), F32),
        jax.ShapeDtypeStruct((VT_ROWS, t), BF16),
        jax.ShapeDtypeStruct((t, IDX_DIM), F32),
        jax.ShapeDtypeStruct((t, IDX_HEADS * IDX_DIM), BF16),
        jax.ShapeDtypeStruct((t, IDX_HEADS * IDX_DIM), BF16),
        jax.ShapeDtypeStruct((t, LANES), F32),
        jax.ShapeDtypeStruct((t, MEM_WIDTH), BF16),
    ]
    out_specs = [
        row(POOL_WIDTH),
        pl.BlockSpec((N_HEADS, tm, LANES), lambda i: (0, i, 0)),
        row(LANES), row(LANES), row(LANES),
        pl.BlockSpec((VT_ROWS, tm), lambda i: (0, i)),
        row(IDX_DIM), row(IDX_HEADS * IDX_DIM), row(IDX_HEADS * IDX_DIM), row(LANES), row(MEM_WIDTH),
    ]
    return pl.pallas_call(
        _proj_body,
        grid=(t // tm,),
        in_specs=[row(D_MODEL), tab, tab, tab, tab, tab, tab, _full((1, D_MODEL)),
                  _resident((D_MODEL, PROJ_W)), _full((1, LANES)), _full((1, LANES)),
                  _full((1, LANES)), _full((LANES, LANES))],
        out_specs=out_specs,
        out_shape=out_shape,
        compiler_params=_params(("arbitrary",)),
        name="input_projection",
    )(x, cq, saq, sbq, ci, sai, sbi, norm1.reshape(1, D_MODEL), wp,
      jnp.tile(q_norm, 2).reshape(1, LANES), jnp.tile(k_norm, 2).reshape(1, LANES),
      jnp.tile(mem_q_norm, 2).reshape(1, LANES), bmat)


def _pool_mem_body(u_ref, hist_ref, pw_ref, ps_ref, qm_ref, mk_ref, mv_ref,
                   apool_ref, amem_ref, carry_ref, *, tiles_per_seq, pos0):
    j = pl.program_id(0)
    tm = u_ref.shape[0]
    tile_in_seq = j % tiles_per_seq

    @pl.when(tile_in_seq == 0)
    def _():
        carry_ref[...] = hist_ref[0]

    u = u_ref[...]
    ext = jnp.concatenate([carry_ref[...], u], axis=0)
    carry_ref[...] = u[tm - 2 * SUBLANES:, :]
    s2 = ext + pltpu.roll(ext, 1, axis=0)
    s4 = s2 + pltpu.roll(s2, 2, axis=0)
    s8 = s4 + pltpu.roll(s4, 4, axis=0)
    s16 = s8 + pltpu.roll(s8, 8, axis=0)
    hs = 2 * SUBLANES
    gw = POOL_WIDTH // POOL_GROUPS
    lane = lax.broadcasted_iota(I32, (1, POOL_WIDTH), 1)
    win_sum = jnp.where(lane < gw, s2[hs:], jnp.where(lane < 2 * gw, s4[hs:],
                        jnp.where(lane < 3 * gw, s8[hs:], s16[hs:])))
    win = jnp.where(lane < gw, POOL_WINDOWS[0], jnp.where(lane < 2 * gw, POOL_WINDOWS[1],
                    jnp.where(lane < 3 * gw, POOL_WINDOWS[2], POOL_WINDOWS[3])))
    pos = pos0 + tile_in_seq * tm + lax.broadcasted_iota(I32, (tm, 1), 0)
    cnt = jnp.minimum(win, pos + 1).astype(F32)
    z = win_sum / cnt - u
    y = _dot(z.astype(BF16), pw_ref[...]) * ps_ref[...]
    apool_ref[...] = y.astype(BF16)

    qm = qm_ref[...]
    lane_m = lax.broadcasted_iota(I32, (1, MEM_WIDTH), 1)
    mk = mk_ref[0].astype(BF16)
    mv = mv_ref[0].astype(BF16)
    zero = jnp.zeros_like(qm)
    qs = jnp.concatenate(
        [jnp.where((lane_m >= h * HEAD_DIM) & (lane_m < (h + 1) * HEAD_DIM), qm, zero)
         for h in range(MEM_HEADS)], axis=0)
    logits = _dot_nt(qs, mk)
    mx = jnp.max(logits, axis=-1, keepdims=True)
    p = jnp.exp(logits - mx)
    den = jnp.sum(p, axis=-1, keepdims=True)
    o = _dot(p.astype(BF16), mv) / den
    out = jnp.zeros((tm, MEM_WIDTH), F32)
    for h in range(MEM_HEADS):
        oh = o[h * tm:(h + 1) * tm]
        out = jnp.where((lane_m >= h * HEAD_DIM) & (lane_m < (h + 1) * HEAD_DIM), oh, out)
    amem_ref[...] = out.astype(BF16)


def _pool_mem(u, hist, pw_bd, pool_scale, qm, mk, mv, tm, tiles_per_seq, pos0):
    t = u.shape[0]
    seq = lambda j: j // tiles_per_seq
    return pl.pallas_call(
        functools.partial(_pool_mem_body, tiles_per_seq=tiles_per_seq, pos0=pos0),
        grid=(t // tm,),
        in_specs=[pl.BlockSpec((tm, POOL_WIDTH), lambda j: (j, 0)),
                  pl.BlockSpec((1, 2 * SUBLANES, POOL_WIDTH), lambda j: (seq(j), 0, 0)),
                  _full((POOL_WIDTH, POOL_WIDTH)), _full((1, POOL_WIDTH)),
                  pl.BlockSpec((tm, MEM_WIDTH), lambda j: (j, 0)),
                  pl.BlockSpec((1, MEM_TOKENS, MEM_WIDTH), lambda j: (seq(j) % mk.shape[0], 0, 0)),
                  pl.BlockSpec((1, MEM_TOKENS, MEM_WIDTH), lambda j: (seq(j) % mk.shape[0], 0, 0))],
        out_specs=[pl.BlockSpec((tm, POOL_WIDTH), lambda j: (j, 0)),
                   pl.BlockSpec((tm, MEM_WIDTH), lambda j: (j, 0))],
        out_shape=[jax.ShapeDtypeStruct((t, POOL_WIDTH), BF16),
                   jax.ShapeDtypeStruct((t, MEM_WIDTH), BF16)],
        scratch_shapes=[pltpu.VMEM((2 * SUBLANES, POOL_WIDTH), F32)],
        compiler_params=_params(("arbitrary",)),
        name="pool_and_memory_attention",
    )(u, hist, pw_bd, pool_scale.reshape(1, POOL_WIDTH), qm, mk, mv)


def _midpoint(lo, hi):
    am = 0.5 * lo + 0.5 * hi
    alo, ahi = jnp.abs(lo), jnp.abs(hi)
    small, large = jnp.minimum(alo, ahi), jnp.maximum(alo, ahi)
    sign = jnp.where(hi > 0.0, 1.0, -1.0)
    same = jnp.where(small < large * SCALE_JUMP, sign * large * SCALE_JUMP,
                     jnp.where(large > 4.0 * small, sign * (jnp.sqrt(small) * jnp.sqrt(large)), am))
    straddle = jnp.where(hi >= -lo, hi * SCALE_JUMP, 0.0)
    at_lo0 = jnp.where(hi > TINY, TINY, 0.0)
    at_hi0 = jnp.where(lo < -TINY, -TINY, 0.0)
    return jnp.where((lo < 0.0) & (hi > 0.0), straddle,
                     jnp.where(lo == 0.0, at_lo0, jnp.where(hi == 0.0, at_hi0, same)))


def _sublane_total(x):
    x = x + pltpu.roll(x, 4, axis=0)
    x = x + pltpu.roll(x, 2, axis=0)
    return x + pltpu.roll(x, 1, axis=0)


def _dsa_body(lim_ref, qx_ref, qi_ref, wi_ref, kb_ref, vt_ref, kir_ref, out_ref,
              score_ref, acc_ref, kmax_ref, *, causal, n_tiles_static, blocks_per_seq, top_k):
    qb = pl.program_id(0) % blocks_per_seq
    tk = KTILE
    if causal:
        n_tiles = ((qb + 1) * QBLK + tk - 1) // tk
    else:
        n_tiles = n_tiles_static
    limit = lim_ref[0]

    @pl.when(qb == 0)
    def _():
        ones = jnp.ones((LANES, LANES), BF16)

        def knorm(i, mx):
            kt = kb_ref[pl.ds(pl.multiple_of(i * tk, tk), tk), :].astype(F32)
            return jnp.maximum(mx, jnp.max(_dot((kt * kt).astype(BF16), ones), axis=0, keepdims=True))

        kmax_ref[...] = lax.fori_loop(0, n_tiles_static, knorm, jnp.zeros((1, LANES), F32))

    qi = qi_ref[...]
    lane_i = lax.broadcasted_iota(I32, (1, IDX_HEADS * IDX_DIM), 1)
    zero_qi = jnp.zeros_like(qi)
    qi_rows = jnp.concatenate(
        [jnp.where((lane_i >= h * IDX_DIM) & (lane_i < (h + 1) * IDX_DIM), qi, zero_qi)
         for h in range(IDX_HEADS)], axis=0)
    w_t = wi_ref[...].T
    n_cls = TOPK_MAX

    def score_tile(r0, cls_max):
        s = _dot_nt(kir_ref[pl.ds(r0, tk), :], qi_rows)
        score = jnp.zeros((tk, QBLK), F32)
        for h in range(IDX_HEADS):
            score = score + jnp.maximum(s[:, h * QBLK:(h + 1) * QBLK], 0.0) * w_t[h:h + 1, :]
        kidx = r0 + lax.broadcasted_iota(I32, (tk, 1), 0)
        score = jnp.where(kidx < limit, score, -jnp.inf)
        score_ref[pl.ds(r0, tk), :] = score
        for c in range(tk // n_cls):
            cls_max = jnp.maximum(cls_max, score[c * n_cls:(c + 1) * n_cls])
        return cls_max

    def score_step(i, cls_max):
        for sub in range(KUNROLL):
            cls_max = score_tile(pl.multiple_of((i * KUNROLL + sub) * tk, tk), cls_max)
        return cls_max

    n_steps = (n_tiles + KUNROLL - 1) // KUNROLL
    cls_max = lax.fori_loop(0, n_steps, score_step, jnp.full((n_cls, QBLK), -jnp.inf, F32))
    lo0 = jnp.min(cls_max, axis=0, keepdims=True)
    top = jnp.max(cls_max, axis=0, keepdims=True)
    hi0 = top + (jnp.abs(top) * 2.0 ** -20 + 1e-30)
    few = limit <= top_k
    rep = lambda x: jnp.broadcast_to(x, (SUBLANES, QBLK))
    lo0 = rep(jnp.where(few, ALL_VISIBLE, jnp.maximum(lo0, ALL_VISIBLE)))
    hi0 = rep(hi0)

    def count_ge(thr):
        rows = KUNROLL * tk

        def body(i, acc):
            r0 = pl.multiple_of(i * rows, rows)
            sc = score_ref[pl.ds(r0, rows), :]
            chains = [None] * COUNT_CHAINS
            for g in range(rows // SUBLANES):
                ind = (sc[g * SUBLANES:(g + 1) * SUBLANES] >= thr).astype(I32)
                c = g % COUNT_CHAINS
                chains[c] = ind if chains[c] is None else chains[c] + ind
            while len(chains) > 1:
                chains = [a + b for a, b in zip(chains[0::2], chains[1::2])]
            return acc + chains[0]
        acc = lax.fori_loop(0, n_steps, body, jnp.zeros((SUBLANES, QBLK), I32))
        return _sublane_total(acc)

    def probe(lo, hi, c_lo):
        mid = _midpoint(lo, hi)
        return mid, (c_lo != top_k) & (mid > lo) & (mid < hi)

    def search_step(st):
        lo, hi, c_lo, c_hi = st
        mid, open_ = probe(lo, hi, c_lo)
        c = count_ge(mid)
        up = open_ & (c >= top_k)
        dn = open_ & (c < top_k)
        return (jnp.where(up, mid, lo), jnp.where(dn, mid, hi),
                jnp.where(up, c, c_lo), jnp.where(dn, c, c_hi))

    def search_cond(st):
        step, lo, hi, c_lo, _ = st
        any_open = jnp.max(probe(lo, hi, c_lo)[1][0:1, :].astype(F32)) > 0.0
        return (step < MAX_SEARCH_STEPS) & any_open

    big = jnp.full((SUBLANES, QBLK), 2 * top_k, I32)
    c_lo0 = jnp.where(rep(few), top_k, big)
    st = lax.fori_loop(0, MIN_SEARCH_STEPS, lambda _, st: search_step(st),
                       (lo0, hi0, c_lo0, jnp.zeros_like(big)))
    _, thr, _, c_lo, c_hi = lax.while_loop(
        search_cond, lambda st: (st[0] + 1,) + search_step(st[1:]), (jnp.int32(MIN_SEARCH_STEPS),) + st)

    tied = c_lo != top_k
    need = jnp.where(tied, top_k - c_hi, jnp.int32(2 ** 30))

    @pl.when(jnp.max(tied.astype(I32)) > 0)
    def _():
        sub = lax.broadcasted_iota(I32, (SUBLANES, QBLK), 0)

        def fix(v, seen):
            r0 = pl.multiple_of(v * SUBLANES, SUBLANES)
            sc = score_ref[pl.ds(r0, SUBLANES), :]
            eq = (sc == thr).astype(I32)
            pre = eq
            for sft in (1, 2, 4):
                pre = pre + jnp.where(sub >= sft, pltpu.roll(pre, sft, axis=0), 0)
            rank = seen + pre - eq
            score_ref[pl.ds(r0, SUBLANES), :] = jnp.where((eq > 0) & (rank >= need), -jnp.inf, sc)
            return seen + _sublane_total(eq)

        lax.fori_loop(0, n_tiles * (tk // SUBLANES), fix, jnp.zeros((SUBLANES, QBLK), I32))

    qx = qx_ref[...].reshape(N_HEADS * QBLK, LANES)
    thr_row = thr[0:1, :]
    acc_ref[...] = jnp.zeros_like(acc_ref)

    qf = qx.astype(F32)
    qn2 = _dot_nt(jnp.ones((SUBLANES, LANES), BF16), (qf * qf).astype(BF16))[0:1, :]
    kmax2 = jnp.concatenate([kmax_ref[...]] * N_HEADS, axis=1)
    bound = jnp.sqrt(qn2 * kmax2) * 1.01
    fast = jnp.max(bound) <= FAST_SHIFT_LIMIT

    def finish(o):
        hpk = N_HEADS // N_KV_HEADS
        for g in range(N_HEADS // 2):
            parts = []
            for h in (2 * g, 2 * g + 1):
                jkv = h // hpk
                parts.append(o[jkv * HEAD_DIM:(jkv + 1) * HEAD_DIM, h * QBLK:(h + 1) * QBLK])
            blk = jnp.concatenate(parts, axis=0)
            out_ref[:, g * LANES:(g + 1) * LANES] = blk.T.astype(BF16)

    @pl.when(fast)
    def _():
        def attend_tile(r0):
            logit = _dot_nt(kb_ref[pl.ds(r0, tk), :], qx)
            sel = score_ref[pl.ds(r0, tk), :] >= thr_row
            ps = []
            for h in range(N_HEADS):
                sl = slice(h * QBLK, (h + 1) * QBLK)
                ps.append(jnp.where(sel, jnp.exp2(logit[:, sl] - bound[:, sl]), 0.0).astype(BF16))
            return _dot(vt_ref[:, pl.ds(r0, tk)], jnp.concatenate(ps, axis=1))

        def attend(i, carry):
            pv = attend_tile(pl.multiple_of(i * KUNROLL * tk, tk))
            for sub in range(1, KUNROLL):
                pv = pv + attend_tile(pl.multiple_of((i * KUNROLL + sub) * tk, tk))
            acc_ref[...] += pv
            return carry

        lax.fori_loop(0, n_steps, attend, 0)
        acc = acc_ref[...]
        finish(acc[:LANES] / acc[LANES:LANES + 1])

    @pl.when(jnp.logical_not(fast))
    def _():
        def attend(i, m):
            r0 = pl.multiple_of(i * tk, tk)
            logit = _dot_nt(kb_ref[pl.ds(r0, tk), :], qx)
            sel = score_ref[pl.ds(r0, tk), :] >= thr_row
            m_new = jnp.maximum(m, jnp.max(logit, axis=0, keepdims=True))
            alpha = jnp.exp2(m - m_new)
            ps = []
            for h in range(N_HEADS):
                sl = slice(h * QBLK, (h + 1) * QBLK)
                ps.append(jnp.where(sel, jnp.exp2(logit[:, sl] - m_new[:, sl]), 0.0).astype(BF16))
            p = jnp.concatenate(ps, axis=1)
            acc_ref[...] = acc_ref[...] * alpha + _dot(vt_ref[:, pl.ds(r0, tk)], p)
            return m_new

        lax.fori_loop(0, n_tiles, attend, jnp.full((1, N_HEADS * QBLK), -1e30, F32))
        acc = acc_ref[...]
        finish(acc[:LANES] / acc[LANES:LANES + 1])


def _dsa(limits, qx, qi, wi, kb, vt, kir, *, causal, top_k):
    nb, _, t, _ = qx.shape
    lp = kb.shape[1]
    nq = t // QBLK
    body = functools.partial(_dsa_body, causal=causal, n_tiles_static=lp // KTILE,
                             blocks_per_seq=nq, top_k=top_k)
    bq = lambda i: (i // nq, i % nq)
    return pl.pallas_call(
        body,
        grid=(nb * nq,),
        in_specs=[
            pl.BlockSpec((1, 1, QBLK), lambda i: (i, 0, 0)),
            pl.BlockSpec((None, N_HEADS, QBLK, LANES), lambda i: (bq(i)[0], 0, bq(i)[1], 0)),
            pl.BlockSpec((None, QBLK, IDX_HEADS * IDX_DIM), lambda i: (bq(i)[0], bq(i)[1], 0)),
            pl.BlockSpec((None, QBLK, LANES), lambda i: (bq(i)[0], bq(i)[1], 0)),
            pl.BlockSpec((None, lp, LANES), lambda i: (bq(i)[0], 0, 0), pipeline_mode=pl.Buffered(1)),
            pl.BlockSpec((None, VT_ROWS, lp), lambda i: (bq(i)[0], 0, 0), pipeline_mode=pl.Buffered(1)),
            pl.BlockSpec((None, lp, IDX_HEADS * IDX_DIM), lambda i: (bq(i)[0], 0, 0),
                         pipeline_mode=pl.Buffered(1)),
        ],
        out_specs=pl.BlockSpec((None, QBLK, N_HEADS * HEAD_DIM), lambda i: (bq(i)[0], bq(i)[1], 0)),
        out_shape=jax.ShapeDtypeStruct((nb, t, N_HEADS * HEAD_DIM), BF16),
        scratch_shapes=[pltpu.VMEM((lp, QBLK), F32), pltpu.VMEM((VT_ROWS, N_HEADS * QBLK), F32),
                        pltpu.VMEM((1, LANES), F32)],
        compiler_params=_params(("arbitrary",)),
        name="dsa_attention",
    )(limits, qx, qi, wi, kb, vt, kir)


def _ffn_body(x_ref, ap_ref, aa_ref, am_ref, wo_ref, g2_ref, wup_ref, cw_ref, cb_ref, wdn_ref,
              hist_ref, y_ref, cst_ref, carry_ref, acc_ref, *, tiles_per_seq):
    j = pl.program_id(0)
    tm = x_ref.shape[0]
    tile_in_seq = j % tiles_per_seq

    @pl.when(tile_in_seq == 0)
    def _():
        carry_ref[...] = hist_ref[0]

    mix = jnp.concatenate([ap_ref[...], aa_ref[...], am_ref[...]], axis=1)
    h = x_ref[...] + _dot(mix, wo_ref[...])
    nb = _rms(h, g2_ref[...]).astype(BF16)
    acc_ref[...] = h
    n_chunk = D_FF // FFN_CHUNK
    for c in range(n_chunk):
        conv = []
        for part in range(2):
            c0 = part * D_FF + c * FFN_CHUNK
            cols = slice(c0, c0 + FFN_CHUNK)
            up = _dot(nb, wup_ref[:, cols])
            ext = jnp.concatenate([carry_ref[:, cols], up], axis=0)
            carry_ref[:, cols] = up[tm - SUBLANES:, :]
            cst_ref[0, :, cols] = up[tm - (CONV_W - 1):, :]

            cv = (cb_ref[:, cols] + ext[SUBLANES:] * cw_ref[2:3, cols]
                  + pltpu.roll(ext, 1, axis=0)[SUBLANES:] * cw_ref[1:2, cols]
                  + pltpu.roll(ext, 2, axis=0)[SUBLANES:] * cw_ref[0:1, cols])
            conv.append(cv)
        gate, val = conv
        act = gate / (1.0 + jnp.exp(-gate)) * val
        acc_ref[...] += _dot(act.astype(BF16), wdn_ref[c * FFN_CHUNK:(c + 1) * FFN_CHUNK, :])
    y_ref[...] = acc_ref[...]


def _out_ffn(x, a_pool, a_attn, a_mem, w_out, norm2, w_up, conv_w, conv_b, w_down, hist, tm,
             tiles_per_seq):
    t = x.shape[0]
    n_seq = t // (tm * tiles_per_seq)
    row = lambda w: pl.BlockSpec((tm, w), lambda j: (j, 0))
    seq = lambda j: j // tiles_per_seq
    return pl.pallas_call(
        functools.partial(_ffn_body, tiles_per_seq=tiles_per_seq),
        grid=(t // tm,),
        in_specs=[row(D_MODEL), row(POOL_WIDTH), row(N_HEADS * HEAD_DIM), row(MEM_WIDTH),
                  _resident((D_MODEL, D_MODEL)), _full((1, D_MODEL)),
                  _resident((D_MODEL, 2 * D_FF)), _full((CONV_W, 2 * D_FF)), _full((1, 2 * D_FF)),
                  _resident((D_FF, D_MODEL)),
                  pl.BlockSpec((1, SUBLANES, 2 * D_FF), lambda j: (seq(j), 0, 0))],
        out_specs=[row(D_MODEL),
                   pl.BlockSpec((1, CONV_W - 1, 2 * D_FF), lambda j: (seq(j), 0, 0))],
        out_shape=[jax.ShapeDtypeStruct((t, D_MODEL), F32),
                   jax.ShapeDtypeStruct((n_seq, CONV_W - 1, 2 * D_FF), F32)],
        scratch_shapes=[pltpu.VMEM((SUBLANES, 2 * D_FF), F32), pltpu.VMEM((tm, D_MODEL), F32)],
        compiler_params=_params(("arbitrary",)),
        name="out_proj_conv_ffn",
    )(x, a_pool, a_attn, a_mem, w_out.astype(BF16), norm2.reshape(1, D_MODEL), w_up.astype(BF16),
      conv_w, conv_b.reshape(1, 2 * D_FF), w_down.astype(BF16), hist)


def _pad_rows(a, rows, axis):
    pad = [(0, 0)] * a.ndim
    pad[axis] = (0, rows - a.shape[axis])
    return jnp.pad(a, pad)


def _layer(x, pos, pos0, keys_past, mk, mv, pool_hist, conv_hist, lw, *, causal, tm_proj, tm_pool,
           tm_ffn):
    b, t, _ = x.shape
    xf = x.reshape(b * t, D_MODEL)
    (u, qx, k, kb, v, vt, ki, kir, qi, wi, qm) = _project(
        xf, pos, lw['norm1'], lw['wp'], lw['q_norm'], lw['k_norm'], lw['mem_q_norm'], lw['bmat'],
        tm_proj)

    hist16 = jnp.pad(pool_hist, ((0, 0), (2 * SUBLANES - POOL_HIST, 0), (0, 0)))
    a_pool, a_mem = _pool_mem(u, hist16, lw['pw_bd'], lw['pool_scale'], qm, mk, mv, tm_pool,
                              t // tm_pool, pos0)

    if keys_past is None:
        l_keys = t
        top_k = min(TOPK_MAX, l_keys // 4)
        lp = -(-l_keys // (KTILE * KUNROLL)) * (KTILE * KUNROLL)
        kb_all = _pad_rows(kb.reshape(b, t, LANES), lp, 1)
        vt_all = _pad_rows(vt.reshape(VT_ROWS, b, t).transpose(1, 0, 2), lp, 2)
        kir_all = _pad_rows(kir.reshape(b, t, -1), lp, 1)
        limits = ((jnp.arange(t, dtype=I32) // CHUNK + 1) * CHUNK)
        limits = jnp.tile(limits.reshape(1, t // QBLK, 1, QBLK), (b, 1, 1, 1)).reshape(-1, 1, QBLK)
        qx_b = qx.reshape(N_HEADS, b, t, LANES).transpose(1, 0, 2, 3)
        qi_b = qi.reshape(b, t, -1)
        wi_b = wi.reshape(b, t, LANES)
        a_attn = _dsa(limits, qx_b, qi_b, wi_b, kb_all, vt_all, kir_all, causal=True, top_k=top_k)
        a_attn = a_attn.reshape(b * t, N_HEADS * HEAD_DIM)
    else:
        ck, cv, cki = keys_past
        p_len = ck.shape[1]
        l_keys = p_len + t
        top_k = min(TOPK_MAX, l_keys // 4)
        lp = -(-l_keys // (KTILE * KUNROLL)) * (KTILE * KUNROLL)
        kb_all = _pad_rows(jnp.concatenate([ck.astype(BF16), kb.reshape(b, t, LANES)], axis=1), lp, 1)
        v_new_t = vt.reshape(VT_ROWS, b, t).transpose(1, 0, 2)
        ones_rows = jnp.zeros((b, VT_ROWS - LANES, p_len), BF16).at[:, 0, :].set(1.0)
        cv_t = jnp.concatenate([cv.astype(BF16).transpose(0, 2, 1), ones_rows], axis=1)
        vt_all = _pad_rows(jnp.concatenate([cv_t, v_new_t], axis=2), lp, 2)
        kir_all = _pad_rows(jnp.concatenate(
            [jnp.tile(cki.astype(BF16), (1, 1, IDX_HEADS)), kir.reshape(b, t, -1)], axis=1), lp, 1)
        reps = QBLK // t
        limits = jnp.full((b, 1, QBLK), l_keys, I32)
        qx_b = jnp.tile(qx.reshape(N_HEADS, b, t, LANES).transpose(1, 0, 2, 3), (1, 1, reps, 1))
        qi_b = jnp.tile(qi.reshape(b, t, -1), (1, reps, 1))
        wi_b = jnp.tile(wi.reshape(b, t, LANES), (1, reps, 1))
        a_attn = _dsa(limits, qx_b, qi_b, wi_b, kb_all, vt_all, kir_all, causal=False, top_k=top_k)
        a_attn = a_attn[:, :t].reshape(b * t, N_HEADS * HEAD_DIM)

    hist8 = jnp.pad(conv_hist, ((0, 0), (SUBLANES - (CONV_W - 1), 0), (0, 0)))
    y, conv_state = _out_ffn(xf, a_pool, a_attn, a_mem, lw['w_out'], lw['norm2'], lw['w_up'],
                             lw['conv_w'], lw['conv_b'], lw['w_down'], hist8, tm_ffn, t // tm_ffn)
    y = y.reshape(b, t, D_MODEL)
    k4 = k.reshape(b, t, N_KV_HEADS, HEAD_DIM)
    v4 = v.reshape(b, t, N_KV_HEADS, HEAD_DIM)
    ki3 = ki.reshape(b, t, IDX_DIM)
    u3 = u.reshape(b, t, POOL_WIDTH)
    pool_state = jnp.concatenate([pool_hist, u3], axis=1)[:, -POOL_HIST:]
    return y, k4, v4, ki3, pool_state, conv_state


def kernel(x_prompt, x_sample, mem_prompt, cache_k, cache_v, cache_kidx, cache_mem_k, cache_mem_v,
           state_pool, state_ffn_conv, norm1, w_in, q_norm, k_norm, pool_w, pool_scale, mem_norm,
           w_mem_k, w_mem_v, mem_q_norm, mem_k_norm, w_out, norm2, w_up, conv_w, conv_b, w_down):
    depth = norm1.shape[0]
    bp, sp, _ = x_prompt.shape
    bs, ts, _ = x_sample.shape
    p_len = cache_k.shape[2]
    blk = np.kron(np.eye(LANES // HEAD_DIM), np.ones((HEAD_DIM, HEAD_DIM))) / HEAD_DIM
    bmat = jnp.asarray(blk, BF16)
    xp, xs = x_prompt, x_sample
    p_states, s_states = [], []
    for l in range(depth):
        gw = POOL_WIDTH // POOL_GROUPS
        pw_bd = jnp.zeros((POOL_WIDTH, POOL_WIDTH), F32)
        for g in range(POOL_GROUPS):
            pw_bd = pw_bd.at[g * gw:(g + 1) * gw, g * gw:(g + 1) * gw].set(pool_w[l, g])
        lw = dict(norm1=norm1[l], wp=_prep_w_in(w_in[l]), q_norm=q_norm[l], k_norm=k_norm[l],
                  mem_q_norm=mem_q_norm[l], bmat=bmat, pw_bd=pw_bd.astype(BF16),
                  pool_scale=pool_scale[l], w_out=w_out[l], norm2=norm2[l], w_up=w_up[l],
                  conv_w=conv_w[l], conv_b=conv_b[l], w_down=w_down[l])

        mks, mvs = [], []
        for b in range(bp):
            mk_b, mv_b = _memory_kv(mem_prompt[b], mem_norm[l], w_mem_k[l], w_mem_v[l],
                                    mem_k_norm[l], bmat)
            mks.append(mk_b)
            mvs.append(mv_b)
        mk_p, mv_p = jnp.stack(mks), jnp.stack(mvs)

        xp, k_p, v_p, ki_p, pool_p, conv_p = _layer(
            xp, jnp.arange(sp), 0, None, mk_p, mv_p,
            jnp.zeros((bp, POOL_HIST, POOL_WIDTH), F32), jnp.zeros((bp, CONV_W - 1, 2 * D_FF), F32),
            lw, causal=True, tm_proj=512, tm_pool=512, tm_ffn=512)
        p_states.append((k_p, v_p, ki_p, mk_p.reshape(bp, MEM_TOKENS, MEM_HEADS, HEAD_DIM),
                         mv_p.reshape(bp, MEM_TOKENS, MEM_HEADS, HEAD_DIM), pool_p, conv_p))

        pos_s = jnp.tile(p_len + jnp.arange(ts), bs)
        xs, k_s, v_s, ki_s, pool_s, conv_s = _layer(
            xs, pos_s, p_len,
            (cache_k[l].reshape(bs, p_len, LANES), cache_v[l].reshape(bs, p_len, LANES), cache_kidx[l]),
            cache_mem_k[l].reshape(bs, MEM_TOKENS, MEM_WIDTH),
            cache_mem_v[l].reshape(bs, MEM_TOKENS, MEM_WIDTH),
            state_pool[l], state_ffn_conv[l], lw, causal=False, tm_proj=ts * bs, tm_pool=ts, tm_ffn=ts)
        s_states.append((k_s, v_s, ki_s, pool_s, conv_s))

    k_p, v_p, kidx_p, memk_p, memv_p, pool_p, conv_p = [jnp.stack(z) for z in zip(*p_states)]
    k_s, v_s, kidx_s, pool_s, conv_s = [jnp.stack(z) for z in zip(*s_states)]
    return (xp, xs, k_p, v_p, kidx_p, memk_p, memv_p, pool_p, conv_p, k_s, v_s, kidx_s, pool_s, conv_s)
```

```python
import functools
import math

import numpy as np
import jax
import jax.numpy as jnp
from jax import lax
from jax.experimental import pallas as pl
from jax.experimental.pallas import tpu as pltpu

F32, BF16, I32 = jnp.float32, jnp.bfloat16, jnp.int32

D_MODEL = 1024
CHUNK = 64
HEAD_DIM = 64
N_HEADS = 8
N_KV_HEADS = 2
IDX_HEADS = 8
IDX_DIM = 32
TOPK_MAX = 256
MEM_TOKENS = 256
MEM_HEADS = 4
MEM_WIDTH = MEM_HEADS * HEAD_DIM
POOL_WIDTH = 256
POOL_GROUPS = 4
POOL_WINDOWS = (2, 4, 8, 16)
POOL_HIST = 15
D_FF = 2816
CONV_W = 3
ROPE_THETA = 500000.0
ROT_DIM = HEAD_DIM // 4
IDX_ROT_DIM = IDX_DIM // 4
EPS = 1e-6
SPLITS = (POOL_WIDTH, N_HEADS * HEAD_DIM, N_KV_HEADS * HEAD_DIM, N_KV_HEADS * HEAD_DIM,
          IDX_HEADS * IDX_DIM, IDX_DIM, IDX_HEADS, MEM_WIDTH)

LANES = 128
SUBLANES = 8
VMEM_LIMIT = 56 * 1024 * 1024

COL_U = 0
COL_Q = 256
COL_K = 768
COL_V = 896
COL_QI = 1024
COL_KI = 1280
COL_WI = 1536
COL_QM = 1664
PROJ_W = 1920

QBLK = 128
KTILE = 512
KUNROLL = 4
COUNT_CHAINS = 8
TIE_FIX_GROUPS = 8
LOG2E = 1.4426950408889634
Q_SCALE = HEAD_DIM ** -0.5 * LOG2E
FFN_CHUNK = 256
ALL_VISIBLE = -3.0e38
TINY = 1.1754944e-38
SCALE_JUMP = 2.0 ** -6
MAX_SEARCH_STEPS = 96
MIN_SEARCH_STEPS = 14
VT_ROWS = 144
FAST_SHIFT_LIMIT = 60.0


def _dot(a, b):
    return jnp.dot(a, b, preferred_element_type=F32)


def _dot_nt(a, b):
    return lax.dot_general(a, b, (((1,), (1,)), ((), ())), preferred_element_type=F32)


def _full(shape):
    n = len(shape)
    return pl.BlockSpec(shape, lambda *_: (0,) * n)


def _resident(shape):
    n = len(shape)
    return pl.BlockSpec(shape, lambda *_: (0,) * n, pipeline_mode=pl.Buffered(1))


def _params(sem):
    return pltpu.CompilerParams(dimension_semantics=sem, vmem_limit_bytes=VMEM_LIMIT)


def _rms(x, g):
    ms = jnp.mean(x * x, axis=-1, keepdims=True)
    return x * lax.rsqrt(ms + EPS) * g


def _head_mean_sq(x, bmat):
    sq = x * x
    hi = sq.astype(BF16)
    lo = (sq - hi.astype(F32)).astype(BF16)
    return _dot(hi, bmat) + _dot(lo, bmat)


def _memkv_body(mem_ref, g_ref, wk_ref, wv_ref, gk_ref, bmat_ref, mk_ref, mv_ref):
    m = _rms(mem_ref[...], g_ref[...]).astype(BF16)
    kk = _dot(m, wk_ref[...])
    bmat = bmat_ref[...]
    for c in range(MEM_WIDTH // LANES):
        kc = kk[:, c * LANES:(c + 1) * LANES]
        ms = _head_mean_sq(kc, bmat)
        mk_ref[:, c * LANES:(c + 1) * LANES] = kc * lax.rsqrt(ms + EPS) * gk_ref[...]
    mv_ref[...] = _dot(m, wv_ref[...])


def _memory_kv(mem, mem_norm, w_mem_k, w_mem_v, mem_k_norm, bmat):
    m = mem.shape[0]
    return pl.pallas_call(
        _memkv_body,
        grid=(1,),
        in_specs=[_full((m, D_MODEL)), _full((1, D_MODEL)), _full((D_MODEL, MEM_WIDTH)),
                  _full((D_MODEL, MEM_WIDTH)), _full((1, LANES)), _full((LANES, LANES))],
        out_specs=[_full((m, MEM_WIDTH)), _full((m, MEM_WIDTH))],
        out_shape=[jax.ShapeDtypeStruct((m, MEM_WIDTH), F32)] * 2,
        compiler_params=_params(("arbitrary",)),
        name="memory_kv",
    )(mem, mem_norm.reshape(1, D_MODEL), w_mem_k.astype(BF16), w_mem_v.astype(BF16),
      jnp.tile(mem_k_norm, 2).reshape(1, LANES), bmat)


def _rope(x, c, sa, sb, shift):
    return (x * c + pltpu.roll(x, LANES - shift, axis=1) * sa + pltpu.roll(x, shift, axis=1) * sb)


def _proj_body(x_ref, cq_ref, saq_ref, sbq_ref, ci_ref, sai_ref, sbi_ref, g1_ref, w_ref,
               gq_ref, gk_ref, gm_ref, bmat_ref,
               u_ref, qx_ref, k_ref, kb_ref, v_ref, vt_ref, ki_ref, kir_ref, qi_ref, wi_ref, qm_ref):
    nb = _rms(x_ref[...], g1_ref[...]).astype(BF16)
    proj = _dot(nb, w_ref[...])
    u_ref[...] = proj[:, COL_U:COL_U + POOL_WIDTH]

    bmat = bmat_ref[...]
    cq, saq, sbq = cq_ref[...], saq_ref[...], sbq_ref[...]
    half_q = ROT_DIM // 2

    def head_norm(xc, g):
        ms = _head_mean_sq(xc, bmat)
        return xc * lax.rsqrt(ms + EPS) * g

    lane = lax.broadcasted_iota(I32, (1, LANES), 1)
    low = lane < HEAD_DIM
    n_pair = N_HEADS // 2
    for g in range(n_pair):
        qc = proj[:, COL_Q + g * LANES:COL_Q + (g + 1) * LANES]
        qc = _rope(head_norm(qc, gq_ref[...]), cq, saq, sbq, half_q) * Q_SCALE
        qx_ref[g] = jnp.where(low, qc, 0.0).astype(BF16)
        qx_ref[n_pair + g] = jnp.where(low, 0.0, qc).astype(BF16)

    kc = _rope(head_norm(proj[:, COL_K:COL_K + LANES], gk_ref[...]), cq, saq, sbq, half_q)
    k_ref[...] = kc
    kb_ref[...] = kc.astype(BF16)

    vc = proj[:, COL_V:COL_V + LANES]
    v_ref[...] = vc
    vt_ref[0:LANES, :] = vc.T.astype(BF16)
    extra = lax.broadcasted_iota(I32, (VT_ROWS - LANES, vc.shape[0]), 0)
    vt_ref[LANES:VT_ROWS, :] = jnp.where(extra == 0, 1.0, 0.0).astype(BF16)

    ci, sai, sbi = ci_ref[...], sai_ref[...], sbi_ref[...]
    half_i = IDX_ROT_DIM // 2
    for c in range(IDX_HEADS * IDX_DIM // LANES):
        sl = slice(c * LANES, (c + 1) * LANES)
        qic = _rope(proj[:, COL_QI + c * LANES:COL_QI + (c + 1) * LANES], ci, sai, sbi, half_i)
        qi_ref[:, sl] = (qic * (IDX_DIM ** -0.5)).astype(BF16)
        kic = _rope(proj[:, COL_KI + c * LANES:COL_KI + (c + 1) * LANES], ci, sai, sbi, half_i)
        kir_ref[:, sl] = kic.astype(BF16)
        if c == 0:
            ki_ref[...] = kic[:, :IDX_DIM]

    wi_ref[...] = proj[:, COL_WI:COL_WI + LANES] * (IDX_HEADS ** -0.5)

    for c in range(MEM_WIDTH // LANES):
        qmc = head_norm(proj[:, COL_QM + c * LANES:COL_QM + (c + 1) * LANES], gm_ref[...])
        qm_ref[:, c * LANES:(c + 1) * LANES] = (qmc * (HEAD_DIM ** -0.5)).astype(BF16)


def _rope_tables(pos, rot_dim, period):
    half = rot_dim // 2
    inv = ROPE_THETA ** (-jnp.arange(half, dtype=F32) / half)
    ang = pos.astype(F32)[:, None] * inv[None, :]
    cos, sin = jnp.cos(ang), jnp.sin(ang)
    lane = np.arange(LANES) % period
    first = lane < half
    second = (lane >= half) & (lane < rot_dim)
    idx = np.where(first, lane, np.where(second, lane - half, 0))
    c = jnp.where(jnp.asarray(first | second)[None, :], cos[:, idx], 1.0)
    sa = jnp.where(jnp.asarray(first)[None, :], -sin[:, idx], 0.0)
    sb = jnp.where(jnp.asarray(second)[None, :], sin[:, idx], 0.0)
    return c, sa, sb


def _prep_w_in(w_in):
    cuts = np.cumsum(SPLITS)[:-1]
    wu, wq, wk, wv, wqi, wki, wwi, wqm = jnp.split(w_in, [int(c) for c in cuts], axis=1)
    order = [h for g in range(N_HEADS // 2) for h in (g, N_HEADS // 2 + g)]
    wq = wq.reshape(D_MODEL, N_HEADS, HEAD_DIM)[:, order, :].reshape(D_MODEL, N_HEADS * HEAD_DIM)
    wki = jnp.tile(wki, (1, IDX_HEADS))
    wwi = jnp.pad(wwi, ((0, 0), (0, LANES - IDX_HEADS)))
    return jnp.concatenate([wu, wq, wk, wv, wqi, wki, wwi, wqm], axis=1).astype(BF16)


def _project(x, pos, norm1, wp, q_norm, k_norm, mem_q_norm, bmat, tm):
    t = x.shape[0]
    cq, saq, sbq = _rope_tables(pos, ROT_DIM, HEAD_DIM)
    ci, sai, sbi = _rope_tables(pos, IDX_ROT_DIM, IDX_DIM)
    row = lambda w: pl.BlockSpec((tm, w), lambda i: (i, 0))
    tab = row(LANES)
    out_shape = [
        jax.ShapeDtypeStruct((t, POOL_WIDTH), F32),
        jax.ShapeDtypeStruct((N_HEADS, t, LANES), BF16),
        jax.ShapeDtypeStruct((t, LANES), F32),
        jax.ShapeDtypeStruct((t, LANES), BF16),
        jax.ShapeDtypeStruct((t, LANES), F32),
        jax.ShapeDtypeStruct((VT_ROWS, t), BF16),
        jax.ShapeDtypeStruct((t, IDX_DIM), F32),
        jax.ShapeDtypeStruct((t, IDX_HEADS * IDX_DIM), BF16),
        jax.ShapeDtypeStruct((t, IDX_HEADS * IDX_DIM), BF16),
        jax.ShapeDtypeStruct((t, LANES), F32),
        jax.ShapeDtypeStruct((t, MEM_WIDTH), BF16),
    ]
    out_specs = [
        row(POOL_WIDTH),
        pl.BlockSpec((N_HEADS, tm, LANES), lambda i: (0, i, 0)),
        row(LANES), row(LANES), row(LANES),
        pl.BlockSpec((VT_ROWS, tm), lambda i: (0, i)),
        row(IDX_DIM), row(IDX_HEADS * IDX_DIM), row(IDX_HEADS * IDX_DIM), row(LANES), row(MEM_WIDTH),
    ]
    return pl.pallas_call(
        _proj_body,
        grid=(t // tm,),
        in_specs=[row(D_MODEL), tab, tab, tab, tab, tab, tab, _full((1, D_MODEL)),
                  _resident((D_MODEL, PROJ_W)), _full((1, LANES)), _full((1, LANES)),
                  _full((1, LANES)), _full((LANES, LANES))],
        out_specs=out_specs,
        out_shape=out_shape,
        compiler_params=_params(("arbitrary",)),
        name="input_projection",
    )(x, cq, saq, sbq, ci, sai, sbi, norm1.reshape(1, D_MODEL), wp,
      jnp.tile(q_norm, 2).reshape(1, LANES), jnp.tile(k_norm, 2).reshape(1, LANES),
      jnp.tile(mem_q_norm, 2).reshape(1, LANES), bmat)


def _pool_mem_body(u_ref, hist_ref, pw_ref, ps_ref, qm_ref, mk_ref, mv_ref,
                   apool_ref, amem_ref, carry_ref, *, tiles_per_seq, pos0):
    j = pl.program_id(0)
    tm = u_ref.shape[0]
    tile_in_seq = j % tiles_per_seq

    @pl.when(tile_in_seq == 0)
    def _():
        carry_ref[...] = hist_ref[0]

    u = u_ref[...]
    ext = jnp.concatenate([carry_ref[...], u], axis=0)
    carry_ref[...] = u[tm - 2 * SUBLANES:, :]
    s2 = ext + pltpu.roll(ext, 1, axis=0)
    s4 = s2 + pltpu.roll(s2, 2, axis=0)
    s8 = s4 + pltpu.roll(s4, 4, axis=0)
    s16 = s8 + pltpu.roll(s8, 8, axis=0)
    hs = 2 * SUBLANES
    gw = POOL_WIDTH // POOL_GROUPS
    lane = lax.broadcasted_iota(I32, (1, POOL_WIDTH), 1)
    win_sum = jnp.where(lane < gw, s2[hs:], jnp.where(lane < 2 * gw, s4[hs:],
                        jnp.where(lane < 3 * gw, s8[hs:], s16[hs:])))
    win = jnp.where(lane < gw, POOL_WINDOWS[0], jnp.where(lane < 2 * gw, POOL_WINDOWS[1],
                    jnp.where(lane < 3 * gw, POOL_WINDOWS[2], POOL_WINDOWS[3])))
    pos = pos0 + tile_in_seq * tm + lax.broadcasted_iota(I32, (tm, 1), 0)
    cnt = jnp.minimum(win, pos + 1).astype(F32)
    z = win_sum / cnt - u
    y = _dot(z.astype(BF16), pw_ref[...]) * ps_ref[...]
    apool_ref[...] = y.astype(BF16)

    qm = qm_ref[...]
    lane_m = lax.broadcasted_iota(I32, (1, MEM_WIDTH), 1)
    mk = mk_ref[0].astype(BF16)
    mv = mv_ref[0].astype(BF16)
    zero = jnp.zeros_like(qm)
    qs = jnp.concatenate(
        [jnp.where((lane_m >= h * HEAD_DIM) & (lane_m < (h + 1) * HEAD_DIM), qm, zero)
         for h in range(MEM_HEADS)], axis=0)
    logits = _dot_nt(qs, mk)
    mx = jnp.max(logits, axis=-1, keepdims=True)
    p = jnp.exp(logits - mx)
    den = jnp.sum(p, axis=-1, keepdims=True)
    o = _dot(p.astype(BF16), mv) / den
    out = jnp.zeros((tm, MEM_WIDTH), F32)
    for h in range(MEM_HEADS):
        oh = o[h * tm:(h + 1) * tm]
        out = jnp.where((lane_m >= h * HEAD_DIM) & (lane_m < (h + 1) * HEAD_DIM), oh, out)
    amem_ref[...] = out.astype(BF16)


def _pool_mem(u, hist, pw_bd, pool_scale, qm, mk, mv, tm, tiles_per_seq, pos0):
    t = u.shape[0]
    seq = lambda j: j // tiles_per_seq
    return pl.pallas_call(
        functools.partial(_pool_mem_body, tiles_per_seq=tiles_per_seq, pos0=pos0),
        grid=(t // tm,),
        in_specs=[pl.BlockSpec((tm, POOL_WIDTH), lambda j: (j, 0)),
                  pl.BlockSpec((1, 2 * SUBLANES, POOL_WIDTH), lambda j: (seq(j), 0, 0)),
                  _full((POOL_WIDTH, POOL_WIDTH)), _full((1, POOL_WIDTH)),
                  pl.BlockSpec((tm, MEM_WIDTH), lambda j: (j, 0)),
                  pl.BlockSpec((1, MEM_TOKENS, MEM_WIDTH), lambda j: (seq(j) % mk.shape[0], 0, 0)),
                  pl.BlockSpec((1, MEM_TOKENS, MEM_WIDTH), lambda j: (seq(j) % mk.shape[0], 0, 0))],
        out_specs=[pl.BlockSpec((tm, POOL_WIDTH), lambda j: (j, 0)),
                   pl.BlockSpec((tm, MEM_WIDTH), lambda j: (j, 0))],
        out_shape=[jax.ShapeDtypeStruct((t, POOL_WIDTH), BF16),
                   jax.ShapeDtypeStruct((t, MEM_WIDTH), BF16)],
        scratch_shapes=[pltpu.VMEM((2 * SUBLANES, POOL_WIDTH), F32)],
        compiler_params=_params(("arbitrary",)),
        name="pool_and_memory_attention",
    )(u, hist, pw_bd, pool_scale.reshape(1, POOL_WIDTH), qm, mk, mv)


def _midpoint(lo, hi):
    am = 0.5 * lo + 0.5 * hi
    alo, ahi = jnp.abs(lo), jnp.abs(hi)
    small, large = jnp.minimum(alo, ahi), jnp.maximum(alo, ahi)
    sign = jnp.where(hi > 0.0, 1.0, -1.0)
    same = jnp.where(small < large * SCALE_JUMP, sign * large * SCALE_JUMP,
                     jnp.where(large > 4.0 * small, sign * (jnp.sqrt(small) * jnp.sqrt(large)), am))
    straddle = jnp.where(hi >= -lo, hi * SCALE_JUMP, 0.0)
    at_lo0 = jnp.where(hi > TINY, TINY, 0.0)
    at_hi0 = jnp.where(lo < -TINY, -TINY, 0.0)
    return jnp.where((lo < 0.0) & (hi > 0.0), straddle,
                     jnp.where(lo == 0.0, at_lo0, jnp.where(hi == 0.0, at_hi0, same)))


def _sublane_total(x):
    x = x + pltpu.roll(x, 4, axis=0)
    x = x + pltpu.roll(x, 2, axis=0)
    return x + pltpu.roll(x, 1, axis=0)


def _dsa_body(lim_ref, qx_ref, qi_ref, wi_ref, kb_ref, vt_ref, kir_ref, out_ref,
              score_ref, acc_ref, kmax_ref, *, causal, n_tiles_static, blocks_per_seq, top_k):
    qb = pl.program_id(0) % blocks_per_seq
    tk = KTILE
    if causal:
        n_tiles = ((qb + 1) * QBLK + tk - 1) // tk
    else:
        n_tiles = n_tiles_static
    limit = lim_ref[0]

    @pl.when(qb == 0)
    def _():
        ones = jnp.ones((LANES, LANES), BF16)

        def knorm(i, mx):
            kt = kb_ref[pl.ds(pl.multiple_of(i * tk, tk), tk), :].astype(F32)
            return jnp.maximum(mx, jnp.max(_dot((kt * kt).astype(BF16), ones), axis=0, keepdims=True))

        kmax_ref[...] = lax.fori_loop(0, n_tiles_static, knorm, jnp.zeros((1, LANES), F32))

    qi = qi_ref[...]
    lane_i = lax.broadcasted_iota(I32, (1, IDX_HEADS * IDX_DIM), 1)
    zero_qi = jnp.zeros_like(qi)
    qi_rows = jnp.concatenate(
        [jnp.where((lane_i >= h * IDX_DIM) & (lane_i < (h + 1) * IDX_DIM), qi, zero_qi)
         for h in range(IDX_HEADS)], axis=0)
    w_t = wi_ref[...].T
    n_cls = TOPK_MAX

    def score_tile(r0, cls_max):
        s = _dot_nt(kir_ref[pl.ds(r0, tk), :], qi_rows)
        score = jnp.zeros((tk, QBLK), F32)
        for h in range(IDX_HEADS):
            score = score + jnp.maximum(s[:, h * QBLK:(h + 1) * QBLK], 0.0) * w_t[h:h + 1, :]
        kidx = r0 + lax.broadcasted_iota(I32, (tk, 1), 0)
        score = jnp.where(kidx < limit, score, -jnp.inf)
        score_ref[pl.ds(r0, tk), :] = score
        for c in range(tk // n_cls):
            cls_max = jnp.maximum(cls_max, score[c * n_cls:(c + 1) * n_cls])
        return cls_max

    def score_step(i, cls_max):
        for sub in range(KUNROLL):
            cls_max = score_tile(pl.multiple_of((i * KUNROLL + sub) * tk, tk), cls_max)
        return cls_max

    n_steps = n_tiles // KUNROLL
    tail0 = n_steps * KUNROLL
    cls_max = lax.fori_loop(0, n_steps, score_step, jnp.full((n_cls, QBLK), -jnp.inf, F32))
    cls_max = lax.fori_loop(tail0, n_tiles,
                            lambda i, c: score_tile(pl.multiple_of(i * tk, tk), c), cls_max)
    lo0 = jnp.min(cls_max, axis=0, keepdims=True)
    top = jnp.max(cls_max, axis=0, keepdims=True)
    hi0 = top + (jnp.abs(top) * 2.0 ** -20 + 1e-30)
    few = limit <= top_k
    rep = lambda x: jnp.broadcast_to(x, (SUBLANES, QBLK))
    lo0 = rep(jnp.where(few, ALL_VISIBLE, jnp.maximum(lo0, ALL_VISIBLE)))
    hi0 = rep(hi0)

    def count_ge(thr):
        def count_rows(r0, rows, acc):
            sc = score_ref[pl.ds(r0, rows), :]
            chains = [None] * COUNT_CHAINS
            for g in range(rows // SUBLANES):
                ind = (sc[g * SUBLANES:(g + 1) * SUBLANES] >= thr).astype(I32)
                c = g % COUNT_CHAINS
                chains[c] = ind if chains[c] is None else chains[c] + ind
            while len(chains) > 1:
                chains = [a + b for a, b in zip(chains[0::2], chains[1::2])]
            return acc + chains[0]

        rows = KUNROLL * tk
        acc = lax.fori_loop(0, n_steps,
                            lambda i, a: count_rows(pl.multiple_of(i * rows, rows), rows, a),
                            jnp.zeros((SUBLANES, QBLK), I32))
        acc = lax.fori_loop(tail0, n_tiles,
                            lambda i, a: count_rows(pl.multiple_of(i * tk, tk), tk, a), acc)
        return _sublane_total(acc)

    def probe(lo, hi, c_lo):
        mid = _midpoint(lo, hi)
        return mid, (c_lo != top_k) & (mid > lo) & (mid < hi)

    def search_step(st):
        lo, hi, c_lo, c_hi = st
        mid, open_ = probe(lo, hi, c_lo)
        c = count_ge(mid)
        up = open_ & (c >= top_k)
        dn = open_ & (c < top_k)
        return (jnp.where(up, mid, lo), jnp.where(dn, mid, hi),
                jnp.where(up, c, c_lo), jnp.where(dn, c, c_hi))

    def search_cond(st):
        step, lo, hi, c_lo, _ = st
        any_open = jnp.max(probe(lo, hi, c_lo)[1][0:1, :].astype(F32)) > 0.0
        return (step < MAX_SEARCH_STEPS) & any_open

    big = jnp.full((SUBLANES, QBLK), 2 * top_k, I32)
    c_lo0 = jnp.where(rep(few), top_k, big)
    st = lax.fori_loop(0, MIN_SEARCH_STEPS, lambda _, st: search_step(st),
                       (lo0, hi0, c_lo0, jnp.zeros_like(big)))
    _, thr, _, c_lo, c_hi = lax.while_loop(
        search_cond, lambda st: (st[0] + 1,) + search_step(st[1:]), (jnp.int32(MIN_SEARCH_STEPS),) + st)

    tied = c_lo != top_k
    need = jnp.where(tied, top_k - c_hi, jnp.int32(2 ** 30))

    @pl.when(jnp.max(tied.astype(I32)) > 0)
    def _():
        sub = lax.broadcasted_iota(I32, (SUBLANES, QBLK), 0)

        def fix_rows(r0, seen):
            sc = score_ref[pl.ds(r0, SUBLANES), :]
            eq = (sc == thr).astype(I32)
            pre = eq
            for sft in (1, 2, 4):
                pre = pre + jnp.where(sub >= sft, pltpu.roll(pre, sft, axis=0), 0)
            rank = seen + pre - eq
            score_ref[pl.ds(r0, SUBLANES), :] = jnp.where((eq > 0) & (rank >= need), -jnp.inf, sc)
            return seen + _sublane_total(eq)

        def fix(v, seen):
            for g in range(TIE_FIX_GROUPS):
                seen = fix_rows(pl.multiple_of((v * TIE_FIX_GROUPS + g) * SUBLANES, SUBLANES), seen)
            return seen

        lax.fori_loop(0, n_tiles * (tk // (SUBLANES * TIE_FIX_GROUPS)), fix,
                      jnp.zeros((SUBLANES, QBLK), I32))

    qx = qx_ref[...].reshape(N_HEADS * QBLK, LANES)
    thr_row = thr[0:1, :]
    acc_ref[...] = jnp.zeros_like(acc_ref)

    qf = qx.astype(F32)
    qn2 = _dot_nt(jnp.ones((SUBLANES, LANES), BF16), (qf * qf).astype(BF16))[0:1, :]
    kmax2 = jnp.concatenate([kmax_ref[...]] * N_HEADS, axis=1)
    bound = jnp.sqrt(qn2 * kmax2) * 1.01
    fast = jnp.max(bound) <= FAST_SHIFT_LIMIT

    def finish(o):
        hpk = N_HEADS // N_KV_HEADS
        for g in range(N_HEADS // 2):
            parts = []
            for h in (2 * g, 2 * g + 1):
                jkv = h // hpk
                parts.append(o[jkv * HEAD_DIM:(jkv + 1) * HEAD_DIM, h * QBLK:(h + 1) * QBLK])
            blk = jnp.concatenate(parts, axis=0)
            out_ref[:, g * LANES:(g + 1) * LANES] = blk.T.astype(BF16)

    @pl.when(fast)
    def _():
        def attend_tile(r0):
            logit = _dot_nt(kb_ref[pl.ds(r0, tk), :], qx)
            sel = score_ref[pl.ds(r0, tk), :] >= thr_row
            ps = []
            for h in range(N_HEADS):
                sl = slice(h * QBLK, (h + 1) * QBLK)
                ps.append(jnp.where(sel, jnp.exp2(logit[:, sl] - bound[:, sl]), 0.0).astype(BF16))
            return _dot(vt_ref[:, pl.ds(r0, tk)], jnp.concatenate(ps, axis=1))

        def attend(i, carry):
            pv = attend_tile(pl.multiple_of(i * KUNROLL * tk, tk))
            for sub in range(1, KUNROLL):
                pv = pv + attend_tile(pl.multiple_of((i * KUNROLL + sub) * tk, tk))
            acc_ref[...] += pv
            return carry

        def attend_one(i, carry):
            acc_ref[...] += attend_tile(pl.multiple_of(i * tk, tk))
            return carry

        lax.fori_loop(0, n_steps, attend, 0)
        lax.fori_loop(tail0, n_tiles, attend_one, 0)
        acc = acc_ref[...]
        finish(acc[:LANES] / acc[LANES:LANES + 1])

    @pl.when(jnp.logical_not(fast))
    def _():
        def attend(i, m):
            r0 = pl.multiple_of(i * tk, tk)
            logit = _dot_nt(kb_ref[pl.ds(r0, tk), :], qx)
            sel = score_ref[pl.ds(r0, tk), :] >= thr_row
            m_new = jnp.maximum(m, jnp.max(logit, axis=0, keepdims=True))
            alpha = jnp.exp2(m - m_new)
            ps = []
            for h in range(N_HEADS):
                sl = slice(h * QBLK, (h + 1) * QBLK)
                ps.append(jnp.where(sel, jnp.exp2(logit[:, sl] - m_new[:, sl]), 0.0).astype(BF16))
            p = jnp.concatenate(ps, axis=1)
            acc_ref[...] = acc_ref[...] * alpha + _dot(vt_ref[:, pl.ds(r0, tk)], p)
            return m_new

        lax.fori_loop(0, n_tiles, attend, jnp.full((1, N_HEADS * QBLK), -1e30, F32))
        acc = acc_ref[...]
        finish(acc[:LANES] / acc[LANES:LANES + 1])


def _dsa(limits, qx, qi, wi, kb, vt, kir, *, causal, top_k):
    nb, _, t, _ = qx.shape
    lp = kb.shape[1]
    nq = t // QBLK
    body = functools.partial(_dsa_body, causal=causal, n_tiles_static=lp // KTILE,
                             blocks_per_seq=nq, top_k=top_k)
    bq = lambda i: (i // nq, i % nq)
    return pl.pallas_call(
        body,
        grid=(nb * nq,),
        in_specs=[
            pl.BlockSpec((1, 1, QBLK), lambda i: (i, 0, 0)),
            pl.BlockSpec((None, N_HEADS, QBLK, LANES), lambda i: (bq(i)[0], 0, bq(i)[1], 0)),
            pl.BlockSpec((None, QBLK, IDX_HEADS * IDX_DIM), lambda i: (bq(i)[0], bq(i)[1], 0)),
            pl.BlockSpec((None, QBLK, LANES), lambda i: (bq(i)[0], bq(i)[1], 0)),
            pl.BlockSpec((None, lp, LANES), lambda i: (bq(i)[0], 0, 0), pipeline_mode=pl.Buffered(1)),
            pl.BlockSpec((None, VT_ROWS, lp), lambda i: (bq(i)[0], 0, 0), pipeline_mode=pl.Buffered(1)),
            pl.BlockSpec((None, lp, IDX_HEADS * IDX_DIM), lambda i: (bq(i)[0], 0, 0),
                         pipeline_mode=pl.Buffered(1)),
        ],
        out_specs=pl.BlockSpec((None, QBLK, N_HEADS * HEAD_DIM), lambda i: (bq(i)[0], bq(i)[1], 0)),
        out_shape=jax.ShapeDtypeStruct((nb, t, N_HEADS * HEAD_DIM), BF16),
        scratch_shapes=[pltpu.VMEM((lp, QBLK), F32), pltpu.VMEM((VT_ROWS, N_HEADS * QBLK), F32),
                        pltpu.VMEM((1, LANES), F32)],
        compiler_params=_params(("arbitrary",)),
        name="dsa_attention",
    )(limits, qx, qi, wi, kb, vt, kir)


def _ffn_body(x_ref, ap_ref, aa_ref, am_ref, wo_ref, g2_ref, wup_ref, cw_ref, cb_ref, wdn_ref,
              hist_ref, y_ref, cst_ref, carry_ref, acc_ref, *, tiles_per_seq):
    j = pl.program_id(0)
    tm = x_ref.shape[0]
    tile_in_seq = j % tiles_per_seq

    @pl.when(tile_in_seq == 0)
    def _():
        carry_ref[...] = hist_ref[0]

    mix = jnp.concatenate([ap_ref[...], aa_ref[...], am_ref[...]], axis=1)
    h = x_ref[...] + _dot(mix, wo_ref[...])
    nb = _rms(h, g2_ref[...]).astype(BF16)
    acc_ref[...] = h
    n_chunk = D_FF // FFN_CHUNK
    for c in range(n_chunk):
        conv = []
        for part in range(2):
            c0 = part * D_FF + c * FFN_CHUNK
            cols = slice(c0, c0 + FFN_CHUNK)
            up = _dot(nb, wup_ref[:, cols])
            ext = jnp.concatenate([carry_ref[:, cols], up], axis=0)
            carry_ref[:, cols] = up[tm - SUBLANES:, :]
            cst_ref[0, :, cols] = up[tm - (CONV_W - 1):, :]

            cv = (cb_ref[:, cols] + ext[SUBLANES:] * cw_ref[2:3, cols]
                  + pltpu.roll(ext, 1, axis=0)[SUBLANES:] * cw_ref[1:2, cols]
                  + pltpu.roll(ext, 2, axis=0)[SUBLANES:] * cw_ref[0:1, cols])
            conv.append(cv)
        gate, val = conv
        act = gate / (1.0 + jnp.exp(-gate)) * val
        acc_ref[...] += _dot(act.astype(BF16), wdn_ref[c * FFN_CHUNK:(c + 1) * FFN_CHUNK, :])
    y_ref[...] = acc_ref[...]


def _out_ffn(x, a_pool, a_attn, a_mem, w_out, norm2, w_up, conv_w, conv_b, w_down, hist, tm,
             tiles_per_seq):
    t = x.shape[0]
    n_seq = t // (tm * tiles_per_seq)
    row = lambda w: pl.BlockSpec((tm, w), lambda j: (j, 0))
    seq = lambda j: j // tiles_per_seq
    return pl.pallas_call(
        functools.partial(_ffn_body, tiles_per_seq=tiles_per_seq),
        grid=(t // tm,),
        in_specs=[row(D_MODEL), row(POOL_WIDTH), row(N_HEADS * HEAD_DIM), row(MEM_WIDTH),
                  _resident((D_MODEL, D_MODEL)), _full((1, D_MODEL)),
                  _resident((D_MODEL, 2 * D_FF)), _full((CONV_W, 2 * D_FF)), _full((1, 2 * D_FF)),
                  _resident((D_FF, D_MODEL)),
                  pl.BlockSpec((1, SUBLANES, 2 * D_FF), lambda j: (seq(j), 0, 0))],
        out_specs=[row(D_MODEL),
                   pl.BlockSpec((1, CONV_W - 1, 2 * D_FF), lambda j: (seq(j), 0, 0))],
        out_shape=[jax.ShapeDtypeStruct((t, D_MODEL), F32),
                   jax.ShapeDtypeStruct((n_seq, CONV_W - 1, 2 * D_FF), F32)],
        scratch_shapes=[pltpu.VMEM((SUBLANES, 2 * D_FF), F32), pltpu.VMEM((tm, D_MODEL), F32)],
        compiler_params=_params(("arbitrary",)),
        name="out_proj_conv_ffn",
    )(x, a_pool, a_attn, a_mem, w_out.astype(BF16), norm2.reshape(1, D_MODEL), w_up.astype(BF16),
      conv_w, conv_b.reshape(1, 2 * D_FF), w_down.astype(BF16), hist)


def _pad_rows(a, rows, axis):
    pad = [(0, 0)] * a.ndim
    pad[axis] = (0, rows - a.shape[axis])
    return jnp.pad(a, pad)


def _layer(x, pos, pos0, keys_past, mk, mv, pool_hist, conv_hist, lw, *, causal, tm_proj, tm_pool,
           tm_ffn):
    b, t, _ = x.shape
    xf = x.reshape(b * t, D_MODEL)
    (u, qx, k, kb, v, vt, ki, kir, qi, wi, qm) = _project(
        xf, pos, lw['norm1'], lw['wp'], lw['q_norm'], lw['k_norm'], lw['mem_q_norm'], lw['bmat'],
        tm_proj)

    hist16 = jnp.pad(pool_hist, ((0, 0), (2 * SUBLANES - POOL_HIST, 0), (0, 0)))
    a_pool, a_mem = _pool_mem(u, hist16, lw['pw_bd'], lw['pool_scale'], qm, mk, mv, tm_pool,
                              t // tm_pool, pos0)

    if keys_past is None:
        l_keys = t
        top_k = min(TOPK_MAX, l_keys // 4)
        lp = -(-l_keys // KTILE) * KTILE
        kb_all = _pad_rows(kb.reshape(b, t, LANES), lp, 1)
        vt_all = _pad_rows(vt.reshape(VT_ROWS, b, t).transpose(1, 0, 2), lp, 2)
        kir_all = _pad_rows(kir.reshape(b, t, -1), lp, 1)
        limits = ((jnp.arange(t, dtype=I32) // CHUNK + 1) * CHUNK)
        limits = jnp.tile(limits.reshape(1, t // QBLK, 1, QBLK), (b, 1, 1, 1)).reshape(-1, 1, QBLK)
        qx_b = qx.reshape(N_HEADS, b, t, LANES).transpose(1, 0, 2, 3)
        qi_b = qi.reshape(b, t, -1)
        wi_b = wi.reshape(b, t, LANES)
        a_attn = _dsa(limits, qx_b, qi_b, wi_b, kb_all, vt_all, kir_all, causal=True, top_k=top_k)
        a_attn = a_attn.reshape(b * t, N_HEADS * HEAD_DIM)
    else:
        ck, cv, cki = keys_past
        p_len = ck.shape[1]
        l_keys = p_len + t
        top_k = min(TOPK_MAX, l_keys // 4)
        lp = -(-l_keys // KTILE) * KTILE
        kb_all = _pad_rows(jnp.concatenate([ck.astype(BF16), kb.reshape(b, t, LANES)], axis=1), lp, 1)
        v_new_t = vt.reshape(VT_ROWS, b, t).transpose(1, 0, 2)
        ones_rows = jnp.zeros((b, VT_ROWS - LANES, p_len), BF16).at[:, 0, :].set(1.0)
        cv_t = jnp.concatenate([cv.astype(BF16).transpose(0, 2, 1), ones_rows], axis=1)
        vt_all = _pad_rows(jnp.concatenate([cv_t, v_new_t], axis=2), lp, 2)
        kir_all = _pad_rows(jnp.concatenate(
            [jnp.tile(cki.astype(BF16), (1, 1, IDX_HEADS)), kir.reshape(b, t, -1)], axis=1), lp, 1)
        reps = QBLK // t
        limits = jnp.full((b, 1, QBLK), l_keys, I32)
        qx_b = jnp.tile(qx.reshape(N_HEADS, b, t, LANES).transpose(1, 0, 2, 3), (1, 1, reps, 1))
        qi_b = jnp.tile(qi.reshape(b, t, -1), (1, reps, 1))
        wi_b = jnp.tile(wi.reshape(b, t, LANES), (1, reps, 1))
        a_attn = _dsa(limits, qx_b, qi_b, wi_b, kb_all, vt_all, kir_all, causal=False, top_k=top_k)
        a_attn = a_attn[:, :t].reshape(b * t, N_HEADS * HEAD_DIM)

    hist8 = jnp.pad(conv_hist, ((0, 0), (SUBLANES - (CONV_W - 1), 0), (0, 0)))
    y, conv_state = _out_ffn(xf, a_pool, a_attn, a_mem, lw['w_out'], lw['norm2'], lw['w_up'],
                             lw['conv_w'], lw['conv_b'], lw['w_down'], hist8, tm_ffn, t // tm_ffn)
    y = y.reshape(b, t, D_MODEL)
    k4 = k.reshape(b, t, N_KV_HEADS, HEAD_DIM)
    v4 = v.reshape(b, t, N_KV_HEADS, HEAD_DIM)
    ki3 = ki.reshape(b, t, IDX_DIM)
    u3 = u.reshape(b, t, POOL_WIDTH)
    pool_state = jnp.concatenate([pool_hist, u3], axis=1)[:, -POOL_HIST:]
    return y, k4, v4, ki3, pool_state, conv_state


def kernel(x_prompt, x_sample, mem_prompt, cache_k, cache_v, cache_kidx, cache_mem_k, cache_mem_v,
           state_pool, state_ffn_conv, norm1, w_in, q_norm, k_norm, pool_w, pool_scale, mem_norm,
           w_mem_k, w_mem_v, mem_q_norm, mem_k_norm, w_out, norm2, w_up, conv_w, conv_b, w_down):
    depth = norm1.shape[0]
    bp, sp, _ = x_prompt.shape
    bs, ts, _ = x_sample.shape
    p_len = cache_k.shape[2]
    blk = np.kron(np.eye(LANES // HEAD_DIM), np.ones((HEAD_DIM, HEAD_DIM))) / HEAD_DIM
    bmat = jnp.asarray(blk, BF16)
    xp, xs = x_prompt, x_sample
    p_states, s_states = [], []
    for l in range(depth):
        gw = POOL_WIDTH // POOL_GROUPS
        pw_bd = jnp.zeros((POOL_WIDTH, POOL_WIDTH), F32)
        for g in range(POOL_GROUPS):
            pw_bd = pw_bd.at[g * gw:(g + 1) * gw, g * gw:(g + 1) * gw].set(pool_w[l, g])
        lw = dict(norm1=norm1[l], wp=_prep_w_in(w_in[l]), q_norm=q_norm[l], k_norm=k_norm[l],
                  mem_q_norm=mem_q_norm[l], bmat=bmat, pw_bd=pw_bd.astype(BF16),
                  pool_scale=pool_scale[l], w_out=w_out[l], norm2=norm2[l], w_up=w_up[l],
                  conv_w=conv_w[l], conv_b=conv_b[l], w_down=w_down[l])

        mks, mvs = [], []
        for b in range(bp):
            mk_b, mv_b = _memory_kv(mem_prompt[b], mem_norm[l], w_mem_k[l], w_mem_v[l],
                                    mem_k_norm[l], bmat)
            mks.append(mk_b)
            mvs.append(mv_b)
        mk_p, mv_p = jnp.stack(mks), jnp.stack(mvs)

        xp, k_p, v_p, ki_p, pool_p, conv_p = _layer(
            xp, jnp.arange(sp), 0, None, mk_p, mv_p,
            jnp.zeros((bp, POOL_HIST, POOL_WIDTH), F32), jnp.zeros((bp, CONV_W - 1, 2 * D_FF), F32),
            lw, causal=True, tm_proj=512, tm_pool=512, tm_ffn=512)
        p_states.append((k_p, v_p, ki_p, mk_p.reshape(bp, MEM_TOKENS, MEM_HEADS, HEAD_DIM),
                         mv_p.reshape(bp, MEM_TOKENS, MEM_HEADS, HEAD_DIM), pool_p, conv_p))

        pos_s = jnp.tile(p_len + jnp.arange(ts), bs)
        xs, k_s, v_s, ki_s, pool_s, conv_s = _layer(
            xs, pos_s, p_len,
            (cache_k[l].reshape(bs, p_len, LANES), cache_v[l].reshape(bs, p_len, LANES), cache_kidx[l]),
            cache_mem_k[l].reshape(bs, MEM_TOKENS, MEM_WIDTH),
            cache_mem_v[l].reshape(bs, MEM_TOKENS, MEM_WIDTH),
            state_pool[l], state_ffn_conv[l], lw, causal=False, tm_proj=ts * bs, tm_pool=ts, tm_ffn=ts)
        s_states.append((k_s, v_s, ki_s, pool_s, conv_s))

    k_p, v_p, kidx_p, memk_p, memv_p, pool_p, conv_p = [jnp.stack(z) for z in zip(*p_states)]
    k_s, v_s, kidx_s, pool_s, conv_s = [jnp.stack(z) for z in zip(*s_states)]
    return (xp, xs, k_p, v_p, kidx_p, memk_p, memv_p, pool_p, conv_p, k_s, v_s, kidx_s, pool_s, conv_s)
```

```python
import functools
import math

import numpy as np
import jax
import jax.numpy as jnp
from jax import lax
from jax.experimental import pallas as pl
from jax.experimental.pallas import tpu as pltpu

F32, BF16, I32 = jnp.float32, jnp.bfloat16, jnp.int32

D_MODEL = 1024
CHUNK = 64
HEAD_DIM = 64
N_HEADS = 8
N_KV_HEADS = 2
IDX_HEADS = 8
IDX_DIM = 32
TOPK_MAX = 256
MEM_TOKENS = 256
MEM_HEADS = 4
MEM_WIDTH = MEM_HEADS * HEAD_DIM
POOL_WIDTH = 256
POOL_GROUPS = 4
POOL_WINDOWS = (2, 4, 8, 16)
POOL_HIST = 15
D_FF = 2816
CONV_W = 3
ROPE_THETA = 500000.0
ROT_DIM = HEAD_DIM // 4
IDX_ROT_DIM = IDX_DIM // 4
EPS = 1e-6
SPLITS = (POOL_WIDTH, N_HEADS * HEAD_DIM, N_KV_HEADS * HEAD_DIM, N_KV_HEADS * HEAD_DIM,
          IDX_HEADS * IDX_DIM, IDX_DIM, IDX_HEADS, MEM_WIDTH)

LANES = 128
SUBLANES = 8
VMEM_LIMIT = 56 * 1024 * 1024

COL_U = 0
COL_Q = 256
COL_K = 768
COL_V = 896
COL_QI = 1024
COL_KI = 1280
COL_WI = 1536
COL_QM = 1664
PROJ_W = 1920
ROPE_COLS = 32

QBLK = 128
KTILE = 512
KUNROLL = 4
COUNT_CHAINS = 8
TIE_FIX_GROUPS = 8
LOG2E = 1.4426950408889634
Q_SCALE = HEAD_DIM ** -0.5 * LOG2E
FFN_CHUNK = 256
ALL_VISIBLE = -3.0e38
TINY = 1.1754944e-38
SCALE_JUMP = 2.0 ** -6
MAX_SEARCH_STEPS = 96
MIN_SEARCH_STEPS = 16
VT_ROWS = 144
FAST_SHIFT_LIMIT = 60.0


def _dot(a, b):
    return jnp.dot(a, b, preferred_element_type=F32)


def _dot_nt(a, b):
    return lax.dot_general(a, b, (((1,), (1,)), ((), ())), preferred_element_type=F32)


def _full(shape):
    n = len(shape)
    return pl.BlockSpec(shape, lambda *_: (0,) * n)


def _resident(shape):
    n = len(shape)
    return pl.BlockSpec(shape, lambda *_: (0,) * n, pipeline_mode=pl.Buffered(1))


def _params(sem):
    return pltpu.CompilerParams(dimension_semantics=sem, vmem_limit_bytes=VMEM_LIMIT)


def _rms(x, g):
    ms = jnp.mean(x * x, axis=-1, keepdims=True)
    return x * lax.rsqrt(ms + EPS) * g


def _head_mean_sq(x, bmat):
    sq = x * x
    hi = sq.astype(BF16)
    lo = (sq - hi.astype(F32)).astype(BF16)
    return _dot(hi, bmat) + _dot(lo, bmat)


def _memkv_body(mem_ref, g_ref, wk_ref, wv_ref, gk_ref, bmat_ref, mk_ref, mv_ref):
    m = _rms(mem_ref[...], g_ref[...]).astype(BF16)
    kk = _dot(m, wk_ref[...])
    bmat = bmat_ref[...]
    for c in range(MEM_WIDTH // LANES):
        kc = kk[:, c * LANES:(c + 1) * LANES]
        ms = _head_mean_sq(kc, bmat)
        mk_ref[:, c * LANES:(c + 1) * LANES] = kc * lax.rsqrt(ms + EPS) * gk_ref[...]
    mv_ref[...] = _dot(m, wv_ref[...])


def _memory_kv(mem, mem_norm, w_mem_k, w_mem_v, mem_k_norm, bmat):
    m = mem.shape[0]
    return pl.pallas_call(
        _memkv_body,
        grid=(1,),
        in_specs=[_full((m, D_MODEL)), _full((1, D_MODEL)), _full((D_MODEL, MEM_WIDTH)),
                  _full((D_MODEL, MEM_WIDTH)), _full((1, LANES)), _full((LANES, LANES))],
        out_specs=[_full((m, MEM_WIDTH)), _full((m, MEM_WIDTH))],
        out_shape=[jax.ShapeDtypeStruct((m, MEM_WIDTH), F32)] * 2,
        compiler_params=_params(("arbitrary",)),
        name="memory_kv",
    )(mem, mem_norm.reshape(1, D_MODEL), w_mem_k.astype(BF16), w_mem_v.astype(BF16),
      jnp.tile(mem_k_norm, 2).reshape(1, LANES), bmat)


def _rope(x, c, sa, sb, shift):
    return (x * c + pltpu.roll(x, LANES - shift, axis=1) * sa + pltpu.roll(x, shift, axis=1) * sb)


def _proj_body(x_ref, tab_ref, e_ref, g1_ref, w_ref,
               gq_ref, gk_ref, gm_ref, bmat_ref,
               u_ref, qx_ref, k_ref, kb_ref, v_ref, vt_ref, ki_ref, kir_ref, qi_ref, wi_ref, qm_ref):
    nb = _rms(x_ref[...], g1_ref[...]).astype(BF16)
    proj = _dot(nb, w_ref[...])
    u_ref[...] = proj[:, COL_U:COL_U + POOL_WIDTH]

    bmat = bmat_ref[...]
    tabs = _expand_rope(tab_ref[...], e_ref[...])
    cq, saq, sbq, ci, sai, sbi = [tabs[:, i * LANES:(i + 1) * LANES] for i in range(6)]
    half_q = ROT_DIM // 2

    def head_norm(xc, g):
        ms = _head_mean_sq(xc, bmat)
        return xc * lax.rsqrt(ms + EPS) * g

    lane = lax.broadcasted_iota(I32, (1, LANES), 1)
    low = lane < HEAD_DIM
    n_pair = N_HEADS // 2
    for g in range(n_pair):
        qc = proj[:, COL_Q + g * LANES:COL_Q + (g + 1) * LANES]
        qc = _rope(head_norm(qc, gq_ref[...]), cq, saq, sbq, half_q) * Q_SCALE
        qx_ref[g] = jnp.where(low, qc, 0.0).astype(BF16)
        qx_ref[n_pair + g] = jnp.where(low, 0.0, qc).astype(BF16)

    kc = _rope(head_norm(proj[:, COL_K:COL_K + LANES], gk_ref[...]), cq, saq, sbq, half_q)
    k_ref[...] = kc
    kb_ref[...] = kc.astype(BF16)

    vc = proj[:, COL_V:COL_V + LANES]
    v_ref[...] = vc
    vt_ref[0:LANES, :] = vc.T.astype(BF16)
    extra = lax.broadcasted_iota(I32, (VT_ROWS - LANES, vc.shape[0]), 0)
    vt_ref[LANES:VT_ROWS, :] = jnp.where(extra == 0, 1.0, 0.0).astype(BF16)

    half_i = IDX_ROT_DIM // 2
    for c in range(IDX_HEADS * IDX_DIM // LANES):
        sl = slice(c * LANES, (c + 1) * LANES)
        qic = _rope(proj[:, COL_QI + c * LANES:COL_QI + (c + 1) * LANES], ci, sai, sbi, half_i)
        qi_ref[:, sl] = (qic * (IDX_DIM ** -0.5)).astype(BF16)
        kic = _rope(proj[:, COL_KI + c * LANES:COL_KI + (c + 1) * LANES], ci, sai, sbi, half_i)
        kir_ref[:, sl] = kic.astype(BF16)
        if c == 0:
            ki_ref[...] = kic[:, :IDX_DIM]

    wi_ref[...] = proj[:, COL_WI:COL_WI + LANES] * (IDX_HEADS ** -0.5)

    for c in range(MEM_WIDTH // LANES):
        qmc = head_norm(proj[:, COL_QM + c * LANES:COL_QM + (c + 1) * LANES], gm_ref[...])
        qm_ref[:, c * LANES:(c + 1) * LANES] = (qmc * (HEAD_DIM ** -0.5)).astype(BF16)


def _rope_angles(pos):
    cols = []
    for rot_dim in (ROT_DIM, IDX_ROT_DIM):
        half = rot_dim // 2
        inv = ROPE_THETA ** (-jnp.arange(half, dtype=F32) / half)
        ang = pos.astype(F32)[:, None] * inv[None, :]
        cols += [jnp.cos(ang), jnp.sin(ang)]
    used = ROT_DIM + IDX_ROT_DIM
    cols += [jnp.ones((pos.shape[0], 1), F32), jnp.zeros((pos.shape[0], ROPE_COLS - used - 1), F32)]
    return jnp.concatenate(cols, axis=1)


def _rope_expansion():
    e = np.zeros((ROPE_COLS, 6 * LANES), np.float32)
    ones_row = ROT_DIM + IDX_ROT_DIM
    base = 0
    for fam, (rot_dim, period) in enumerate(((ROT_DIM, HEAD_DIM), (IDX_ROT_DIM, IDX_DIM))):
        half = rot_dim // 2
        for l in range(LANES):
            r = l % period
            c_col, sa_col, sb_col = (3 * fam) * LANES + l, (3 * fam + 1) * LANES + l, (3 * fam + 2) * LANES + l
            if r < half:
                e[base + r, c_col] = 1.0
                e[base + half + r, sa_col] = -1.0
            elif r < rot_dim:
                e[base + r - half, c_col] = 1.0
                e[base + half + r - half, sb_col] = 1.0
            else:
                e[ones_row, c_col] = 1.0
        base += rot_dim
    return jnp.asarray(e, BF16)


def _expand_rope(tab, e):
    t1 = tab.astype(BF16)
    r1 = tab - t1.astype(F32)
    t2 = r1.astype(BF16)
    t3 = (r1 - t2.astype(F32)).astype(BF16)
    return _dot(t1, e) + _dot(t2, e) + _dot(t3, e)


def _prep_w_in(w_in):
    cuts = np.cumsum(SPLITS)[:-1]
    wu, wq, wk, wv, wqi, wki, wwi, wqm = jnp.split(w_in, [int(c) for c in cuts], axis=1)
    order = [h for g in range(N_HEADS // 2) for h in (g, N_HEADS // 2 + g)]
    wq = wq.reshape(D_MODEL, N_HEADS, HEAD_DIM)[:, order, :].reshape(D_MODEL, N_HEADS * HEAD_DIM)
    wki = jnp.tile(wki, (1, IDX_HEADS))
    wwi = jnp.pad(wwi, ((0, 0), (0, LANES - IDX_HEADS)))
    return jnp.concatenate([wu, wq, wk, wv, wqi, wki, wwi, wqm], axis=1).astype(BF16)


def _project(x, pos, norm1, wp, q_norm, k_norm, mem_q_norm, bmat, tm):
    t = x.shape[0]
    row = lambda w: pl.BlockSpec((tm, w), lambda i: (i, 0))
    out_shape = [
        jax.ShapeDtypeStruct((t, POOL_WIDTH), F32),
        jax.ShapeDtypeStruct((N_HEADS, t, LANES), BF16),
        jax.ShapeDtypeStruct((t, LANES), F32),
        jax.ShapeDtypeStruct((t, LANES), BF16),
        jax.ShapeDtypeStruct((t, LANES), F32),
        jax.ShapeDtypeStruct((VT_ROWS, t), BF16),
        jax.ShapeDtypeStruct((t, IDX_DIM), F32),
        jax.ShapeDtypeStruct((t, IDX_HEADS * IDX_DIM), BF16),
        jax.ShapeDtypeStruct((t, IDX_HEADS * IDX_DIM), BF16),
        jax.ShapeDtypeStruct((t, LANES), F32),
        jax.ShapeDtypeStruct((t, MEM_WIDTH), BF16),
    ]
    out_specs = [
        row(POOL_WIDTH),
        pl.BlockSpec((N_HEADS, tm, LANES), lambda i: (0, i, 0)),
        row(LANES), row(LANES), row(LANES),
        pl.BlockSpec((VT_ROWS, tm), lambda i: (0, i)),
        row(IDX_DIM), row(IDX_HEADS * IDX_DIM), row(IDX_HEADS * IDX_DIM), row(LANES), row(MEM_WIDTH),
    ]
    return pl.pallas_call(
        _proj_body,
        grid=(t // tm,),
        in_specs=[row(D_MODEL), row(ROPE_COLS), _full((ROPE_COLS, 6 * LANES)), _full((1, D_MODEL)),
                  _resident((D_MODEL, PROJ_W)), _full((1, LANES)), _full((1, LANES)),
                  _full((1, LANES)), _full((LANES, LANES))],
        out_specs=out_specs,
        out_shape=out_shape,
        compiler_params=_params(("arbitrary",)),
        name="input_projection",
    )(x, _rope_angles(pos), _rope_expansion(), norm1.reshape(1, D_MODEL), wp,
      jnp.tile(q_norm, 2).reshape(1, LANES), jnp.tile(k_norm, 2).reshape(1, LANES),
      jnp.tile(mem_q_norm, 2).reshape(1, LANES), bmat)


def _pool_mem_body(u_ref, hist_ref, pw_ref, ps_ref, qm_ref, mk_ref, mv_ref,
                   apool_ref, amem_ref, carry_ref, *, tiles_per_seq, pos0):
    j = pl.program_id(0)
    tm = u_ref.shape[0]
    tile_in_seq = j % tiles_per_seq

    @pl.when(tile_in_seq == 0)
    def _():
        carry_ref[...] = hist_ref[0]

    u = u_ref[...]
    ext = jnp.concatenate([carry_ref[...], u], axis=0)
    carry_ref[...] = u[tm - 2 * SUBLANES:, :]
    s2 = ext + pltpu.roll(ext, 1, axis=0)
    s4 = s2 + pltpu.roll(s2, 2, axis=0)
    s8 = s4 + pltpu.roll(s4, 4, axis=0)
    s16 = s8 + pltpu.roll(s8, 8, axis=0)
    hs = 2 * SUBLANES
    gw = POOL_WIDTH // POOL_GROUPS
    lane = lax.broadcasted_iota(I32, (1, POOL_WIDTH), 1)
    win_sum = jnp.where(lane < gw, s2[hs:], jnp.where(lane < 2 * gw, s4[hs:],
                        jnp.where(lane < 3 * gw, s8[hs:], s16[hs:])))
    win = jnp.where(lane < gw, POOL_WINDOWS[0], jnp.where(lane < 2 * gw, POOL_WINDOWS[1],
                    jnp.where(lane < 3 * gw, POOL_WINDOWS[2], POOL_WINDOWS[3])))
    pos = pos0 + tile_in_seq * tm + lax.broadcasted_iota(I32, (tm, 1), 0)
    cnt = jnp.minimum(win, pos + 1).astype(F32)
    z = win_sum / cnt - u
    y = _dot(z.astype(BF16), pw_ref[...]) * ps_ref[...]
    apool_ref[...] = y.astype(BF16)

    qm = qm_ref[...]
    lane_m = lax.broadcasted_iota(I32, (1, MEM_WIDTH), 1)
    mk = mk_ref[0].astype(BF16)
    mv = mv_ref[0].astype(BF16)
    zero = jnp.zeros_like(qm)
    qs = jnp.concatenate(
        [jnp.where((lane_m >= h * HEAD_DIM) & (lane_m < (h + 1) * HEAD_DIM), qm, zero)
         for h in range(MEM_HEADS)], axis=0)
    logits = _dot_nt(qs, mk)
    mx = jnp.max(logits, axis=-1, keepdims=True)
    p = jnp.exp(logits - mx)
    den = jnp.sum(p, axis=-1, keepdims=True)
    o = _dot(p.astype(BF16), mv) / den
    out = jnp.zeros((tm, MEM_WIDTH), F32)
    for h in range(MEM_HEADS):
        oh = o[h * tm:(h + 1) * tm]
        out = jnp.where((lane_m >= h * HEAD_DIM) & (lane_m < (h + 1) * HEAD_DIM), oh, out)
    amem_ref[...] = out.astype(BF16)


def _pool_mem(u, hist, pw_bd, pool_scale, qm, mk, mv, tm, tiles_per_seq, pos0):
    t = u.shape[0]
    seq = lambda j: j // tiles_per_seq
    return pl.pallas_call(
        functools.partial(_pool_mem_body, tiles_per_seq=tiles_per_seq, pos0=pos0),
        grid=(t // tm,),
        in_specs=[pl.BlockSpec((tm, POOL_WIDTH), lambda j: (j, 0)),
                  pl.BlockSpec((1, 2 * SUBLANES, POOL_WIDTH), lambda j: (seq(j), 0, 0)),
                  _full((POOL_WIDTH, POOL_WIDTH)), _full((1, POOL_WIDTH)),
                  pl.BlockSpec((tm, MEM_WIDTH), lambda j: (j, 0)),
                  pl.BlockSpec((1, MEM_TOKENS, MEM_WIDTH), lambda j: (seq(j) % mk.shape[0], 0, 0)),
                  pl.BlockSpec((1, MEM_TOKENS, MEM_WIDTH), lambda j: (seq(j) % mk.shape[0], 0, 0))],
        out_specs=[pl.BlockSpec((tm, POOL_WIDTH), lambda j: (j, 0)),
                   pl.BlockSpec((tm, MEM_WIDTH), lambda j: (j, 0))],
        out_shape=[jax.ShapeDtypeStruct((t, POOL_WIDTH), BF16),
                   jax.ShapeDtypeStruct((t, MEM_WIDTH), BF16)],
        scratch_shapes=[pltpu.VMEM((2 * SUBLANES, POOL_WIDTH), F32)],
        compiler_params=_params(("arbitrary",)),
        name="pool_and_memory_attention",
    )(u, hist, pw_bd, pool_scale.reshape(1, POOL_WIDTH), qm, mk, mv)


def _midpoint(lo, hi):
    am = 0.5 * lo + 0.5 * hi
    alo, ahi = jnp.abs(lo), jnp.abs(hi)
    small, large = jnp.minimum(alo, ahi), jnp.maximum(alo, ahi)
    sign = jnp.where(hi > 0.0, 1.0, -1.0)
    same = jnp.where(small < large * SCALE_JUMP, sign * large * SCALE_JUMP,
                     jnp.where(large > 4.0 * small, sign * (jnp.sqrt(small) * jnp.sqrt(large)), am))
    straddle = jnp.where(hi >= -lo, hi * SCALE_JUMP, 0.0)
    at_lo0 = jnp.where(hi > TINY, TINY, 0.0)
    at_hi0 = jnp.where(lo < -TINY, -TINY, 0.0)
    return jnp.where((lo < 0.0) & (hi > 0.0), straddle,
                     jnp.where(lo == 0.0, at_lo0, jnp.where(hi == 0.0, at_hi0, same)))


def _sublane_total(x):
    x = x + pltpu.roll(x, 4, axis=0)
    x = x + pltpu.roll(x, 2, axis=0)
    return x + pltpu.roll(x, 1, axis=0)


def _dsa_body(lim_ref, qx_ref, qi_ref, wi_ref, kb_ref, vt_ref, kir_ref, out_ref,
              score_ref, acc_ref, kmax_ref, *, causal, n_tiles_static, blocks_per_seq, top_k):
    qb = pl.program_id(0) % blocks_per_seq
    tk = KTILE
    if causal:
        n_tiles = ((qb + 1) * QBLK + tk - 1) // tk
    else:
        n_tiles = n_tiles_static
    limit = lim_ref[0]

    @pl.when(qb == 0)
    def _():
        ones = jnp.ones((LANES, LANES), BF16)

        def knorm(i, mx):
            kt = kb_ref[pl.ds(pl.multiple_of(i * tk, tk), tk), :].astype(F32)
            return jnp.maximum(mx, jnp.max(_dot((kt * kt).astype(BF16), ones), axis=0, keepdims=True))

        kmax_ref[...] = lax.fori_loop(0, n_tiles_static, knorm, jnp.zeros((1, LANES), F32))

    qi = qi_ref[...]
    lane_i = lax.broadcasted_iota(I32, (1, IDX_HEADS * IDX_DIM), 1)
    zero_qi = jnp.zeros_like(qi)
    qi_rows = jnp.concatenate(
        [jnp.where((lane_i >= h * IDX_DIM) & (lane_i < (h + 1) * IDX_DIM), qi, zero_qi)
         for h in range(IDX_HEADS)], axis=0)
    w_t = wi_ref[...].T
    n_cls = TOPK_MAX

    def score_tile(r0, cls_max):
        s = _dot_nt(kir_ref[pl.ds(r0, tk), :], qi_rows)
        score = jnp.zeros((tk, QBLK), F32)
        for h in range(IDX_HEADS):
            score = score + jnp.maximum(s[:, h * QBLK:(h + 1) * QBLK], 0.0) * w_t[h:h + 1, :]
        kidx = r0 + lax.broadcasted_iota(I32, (tk, 1), 0)
        score = jnp.where(kidx < limit, score, -jnp.inf)
        score_ref[pl.ds(r0, tk), :] = score
        for c in range(tk // n_cls):
            cls_max = jnp.maximum(cls_max, score[c * n_cls:(c + 1) * n_cls])
        return cls_max

    def score_step(i, cls_max):
        for sub in range(KUNROLL):
            cls_max = score_tile(pl.multiple_of((i * KUNROLL + sub) * tk, tk), cls_max)
        return cls_max

    n_steps = n_tiles // KUNROLL
    tail0 = n_steps * KUNROLL
    cls_max = lax.fori_loop(0, n_steps, score_step, jnp.full((n_cls, QBLK), -jnp.inf, F32))
    cls_max = lax.fori_loop(tail0, n_tiles,
                            lambda i, c: score_tile(pl.multiple_of(i * tk, tk), c), cls_max)
    lo0 = jnp.min(cls_max, axis=0, keepdims=True)
    top = jnp.max(cls_max, axis=0, keepdims=True)
    hi0 = top + (jnp.abs(top) * 2.0 ** -20 + 1e-30)
    few = limit <= top_k
    rep = lambda x: jnp.broadcast_to(x, (SUBLANES, QBLK))
    lo0 = rep(jnp.where(few, ALL_VISIBLE, jnp.maximum(lo0, ALL_VISIBLE)))
    hi0 = rep(hi0)

    def count_ge(thr):
        def count_rows(r0, rows, acc):
            sc = score_ref[pl.ds(r0, rows), :]
            chains = [None] * COUNT_CHAINS
            for g in range(rows // SUBLANES):
                ind = (sc[g * SUBLANES:(g + 1) * SUBLANES] >= thr).astype(I32)
                c = g % COUNT_CHAINS
                chains[c] = ind if chains[c] is None else chains[c] + ind
            while len(chains) > 1:
                chains = [a + b for a, b in zip(chains[0::2], chains[1::2])]
            return acc + chains[0]

        rows = KUNROLL * tk
        acc = lax.fori_loop(0, n_steps,
                            lambda i, a: count_rows(pl.multiple_of(i * rows, rows), rows, a),
                            jnp.zeros((SUBLANES, QBLK), I32))
        acc = lax.fori_loop(tail0, n_tiles,
                            lambda i, a: count_rows(pl.multiple_of(i * tk, tk), tk, a), acc)
        return _sublane_total(acc)

    def probe(lo, hi, c_lo):
        mid = _midpoint(lo, hi)
        return mid, (c_lo != top_k) & (mid > lo) & (mid < hi)

    def search_step(st):
        lo, hi, c_lo, c_hi = st
        mid, open_ = probe(lo, hi, c_lo)
        c = count_ge(mid)
        up = open_ & (c >= top_k)
        dn = open_ & (c < top_k)
        return (jnp.where(up, mid, lo), jnp.where(dn, mid, hi),
                jnp.where(up, c, c_lo), jnp.where(dn, c, c_hi))

    def search_cond(st):
        step, lo, hi, c_lo, _ = st
        any_open = jnp.max(probe(lo, hi, c_lo)[1][0:1, :].astype(F32)) > 0.0
        return (step < MAX_SEARCH_STEPS) & any_open

    big = jnp.full((SUBLANES, QBLK), 2 * top_k, I32)
    c_lo0 = jnp.where(rep(few), top_k, big)
    st = lax.fori_loop(0, MIN_SEARCH_STEPS, lambda _, st: search_step(st),
                       (lo0, hi0, c_lo0, jnp.zeros_like(big)))
    _, thr, _, c_lo, c_hi = lax.while_loop(
        search_cond, lambda st: (st[0] + 1,) + search_step(st[1:]), (jnp.int32(MIN_SEARCH_STEPS),) + st)

    tied = c_lo != top_k
    need = jnp.where(tied, top_k - c_hi, jnp.int32(2 ** 30))

    @pl.when(jnp.max(tied.astype(I32)) > 0)
    def _():
        sub = lax.broadcasted_iota(I32, (SUBLANES, QBLK), 0)

        def fix_rows(r0, seen):
            sc = score_ref[pl.ds(r0, SUBLANES), :]
            eq = (sc == thr).astype(I32)
            pre = eq
            for sft in (1, 2, 4):
                pre = pre + jnp.where(sub >= sft, pltpu.roll(pre, sft, axis=0), 0)
            rank = seen + pre - eq
            score_ref[pl.ds(r0, SUBLANES), :] = jnp.where((eq > 0) & (rank >= need), -jnp.inf, sc)
            return seen + _sublane_total(eq)

        def fix(v, seen):
            for g in range(TIE_FIX_GROUPS):
                seen = fix_rows(pl.multiple_of((v * TIE_FIX_GROUPS + g) * SUBLANES, SUBLANES), seen)
            return seen

        lax.fori_loop(0, n_tiles * (tk // (SUBLANES * TIE_FIX_GROUPS)), fix,
                      jnp.zeros((SUBLANES, QBLK), I32))

    qx = qx_ref[...].reshape(N_HEADS * QBLK, LANES)
    thr_row = thr[0:1, :]
    acc_ref[...] = jnp.zeros_like(acc_ref)

    qf = qx.astype(F32)
    qn2 = _dot_nt(jnp.ones((SUBLANES, LANES), BF16), (qf * qf).astype(BF16))[0:1, :]
    kmax2 = jnp.concatenate([kmax_ref[...]] * N_HEADS, axis=1)
    bound = jnp.sqrt(qn2 * kmax2) * 1.01
    fast = jnp.max(bound) <= FAST_SHIFT_LIMIT

    def finish(o):
        hpk = N_HEADS // N_KV_HEADS
        for g in range(N_HEADS // 2):
            parts = []
            for h in (2 * g, 2 * g + 1):
                jkv = h // hpk
                parts.append(o[jkv * HEAD_DIM:(jkv + 1) * HEAD_DIM, h * QBLK:(h + 1) * QBLK])
            blk = jnp.concatenate(parts, axis=0)
            out_ref[:, g * LANES:(g + 1) * LANES] = blk.T.astype(BF16)

    @pl.when(fast)
    def _():
        def attend_tile(r0):
            logit = _dot_nt(kb_ref[pl.ds(r0, tk), :], qx)
            sel = score_ref[pl.ds(r0, tk), :] >= thr_row
            ps = []
            for h in range(N_HEADS):
                sl = slice(h * QBLK, (h + 1) * QBLK)
                ps.append(jnp.where(sel, jnp.exp2(logit[:, sl] - bound[:, sl]), 0.0).astype(BF16))
            return _dot(vt_ref[:, pl.ds(r0, tk)], jnp.concatenate(ps, axis=1))

        def attend(i, carry):
            pv = attend_tile(pl.multiple_of(i * KUNROLL * tk, tk))
            for sub in range(1, KUNROLL):
                pv = pv + attend_tile(pl.multiple_of((i * KUNROLL + sub) * tk, tk))
            acc_ref[...] += pv
            return carry

        def attend_one(i, carry):
            acc_ref[...] += attend_tile(pl.multiple_of(i * tk, tk))
            return carry

        lax.fori_loop(0, n_steps, attend, 0)
        lax.fori_loop(tail0, n_tiles, attend_one, 0)
        acc = acc_ref[...]
        finish(acc[:LANES] / acc[LANES:LANES + 1])

    @pl.when(jnp.logical_not(fast))
    def _():
        def attend(i, m):
            r0 = pl.multiple_of(i * tk, tk)
            logit = _dot_nt(kb_ref[pl.ds(r0, tk), :], qx)
            sel = score_ref[pl.ds(r0, tk), :] >= thr_row
            m_new = jnp.maximum(m, jnp.max(logit, axis=0, keepdims=True))
            alpha = jnp.exp2(m - m_new)
            ps = []
            for h in range(N_HEADS):
                sl = slice(h * QBLK, (h + 1) * QBLK)
                ps.append(jnp.where(sel, jnp.exp2(logit[:, sl] - m_new[:, sl]), 0.0).astype(BF16))
            p = jnp.concatenate(ps, axis=1)
            acc_ref[...] = acc_ref[...] * alpha + _dot(vt_ref[:, pl.ds(r0, tk)], p)
            return m_new

        lax.fori_loop(0, n_tiles, attend, jnp.full((1, N_HEADS * QBLK), -1e30, F32))
        acc = acc_ref[...]
        finish(acc[:LANES] / acc[LANES:LANES + 1])


def _dsa(limits, qx, qi, wi, kb, vt, kir, *, causal, top_k):
    nb, _, t, _ = qx.shape
    lp = kb.shape[1]
    nq = t // QBLK
    body = functools.partial(_dsa_body, causal=causal, n_tiles_static=lp // KTILE,
                             blocks_per_seq=nq, top_k=top_k)
    bq = lambda i: (i // nq, i % nq)
    return pl.pallas_call(
        body,
        grid=(nb * nq,),
        in_specs=[
            pl.BlockSpec((1, 1, QBLK), lambda i: (i, 0, 0)),
            pl.BlockSpec((None, N_HEADS, QBLK, LANES), lambda i: (bq(i)[0], 0, bq(i)[1], 0)),
            pl.BlockSpec((None, QBLK, IDX_HEADS * IDX_DIM), lambda i: (bq(i)[0], bq(i)[1], 0)),
            pl.BlockSpec((None, QBLK, LANES), lambda i: (bq(i)[0], bq(i)[1], 0)),
            pl.BlockSpec((None, lp, LANES), lambda i: (bq(i)[0], 0, 0), pipeline_mode=pl.Buffered(1)),
            pl.BlockSpec((None, VT_ROWS, lp), lambda i: (bq(i)[0], 0, 0), pipeline_mode=pl.Buffered(1)),
            pl.BlockSpec((None, lp, IDX_HEADS * IDX_DIM), lambda i: (bq(i)[0], 0, 0),
                         pipeline_mode=pl.Buffered(1)),
        ],
        out_specs=pl.BlockSpec((None, QBLK, N_HEADS * HEAD_DIM), lambda i: (bq(i)[0], bq(i)[1], 0)),
        out_shape=jax.ShapeDtypeStruct((nb, t, N_HEADS * HEAD_DIM), BF16),
        scratch_shapes=[pltpu.VMEM((lp, QBLK), F32), pltpu.VMEM((VT_ROWS, N_HEADS * QBLK), F32),
                        pltpu.VMEM((1, LANES), F32)],
        compiler_params=_params(("arbitrary",)),
        name="dsa_attention",
    )(limits, qx, qi, wi, kb, vt, kir)


def _ffn_body(x_ref, ap_ref, aa_ref, am_ref, wo_ref, g2_ref, wup_ref, cw_ref, cb_ref, wdn_ref,
              hist_ref, y_ref, cst_ref, carry_ref, acc_ref, *, tiles_per_seq):
    j = pl.program_id(0)
    tm = x_ref.shape[0]
    tile_in_seq = j % tiles_per_seq

    @pl.when(tile_in_seq == 0)
    def _():
        carry_ref[...] = hist_ref[0]

    mix = jnp.concatenate([ap_ref[...], aa_ref[...], am_ref[...]], axis=1)
    h = x_ref[...] + _dot(mix, wo_ref[...])
    nb = _rms(h, g2_ref[...]).astype(BF16)
    acc_ref[...] = h
    n_chunk = D_FF // FFN_CHUNK
    for c in range(n_chunk):
        conv = []
        for part in range(2):
            c0 = part * D_FF + c * FFN_CHUNK
            cols = slice(c0, c0 + FFN_CHUNK)
            up = _dot(nb, wup_ref[:, cols])
            ext = jnp.concatenate([carry_ref[:, cols], up], axis=0)
            carry_ref[:, cols] = up[tm - SUBLANES:, :]
            cst_ref[0, :, cols] = up[tm - (CONV_W - 1):, :]

            cv = (cb_ref[:, cols] + ext[SUBLANES:] * cw_ref[2:3, cols]
                  + pltpu.roll(ext, 1, axis=0)[SUBLANES:] * cw_ref[1:2, cols]
                  + pltpu.roll(ext, 2, axis=0)[SUBLANES:] * cw_ref[0:1, cols])
            conv.append(cv)
        gate, val = conv
        act = gate / (1.0 + jnp.exp(-gate)) * val
        acc_ref[...] += _dot(act.astype(BF16), wdn_ref[c * FFN_CHUNK:(c + 1) * FFN_CHUNK, :])
    y_ref[...] = acc_ref[...]


def _out_ffn(x, a_pool, a_attn, a_mem, w_out, norm2, w_up, conv_w, conv_b, w_down, hist, tm,
             tiles_per_seq):
    t = x.shape[0]
    n_seq = t // (tm * tiles_per_seq)
    row = lambda w: pl.BlockSpec((tm, w), lambda j: (j, 0))
    seq = lambda j: j // tiles_per_seq
    return pl.pallas_call(
        functools.partial(_ffn_body, tiles_per_seq=tiles_per_seq),
        grid=(t // tm,),
        in_specs=[row(D_MODEL), row(POOL_WIDTH), row(N_HEADS * HEAD_DIM), row(MEM_WIDTH),
                  _resident((D_MODEL, D_MODEL)), _full((1, D_MODEL)),
                  _resident((D_MODEL, 2 * D_FF)), _full((CONV_W, 2 * D_FF)), _full((1, 2 * D_FF)),
                  _resident((D_FF, D_MODEL)),
                  pl.BlockSpec((1, SUBLANES, 2 * D_FF), lambda j: (seq(j), 0, 0))],
        out_specs=[row(D_MODEL),
                   pl.BlockSpec((1, CONV_W - 1, 2 * D_FF), lambda j: (seq(j), 0, 0))],
        out_shape=[jax.ShapeDtypeStruct((t, D_MODEL), F32),
                   jax.ShapeDtypeStruct((n_seq, CONV_W - 1, 2 * D_FF), F32)],
        scratch_shapes=[pltpu.VMEM((SUBLANES, 2 * D_FF), F32), pltpu.VMEM((tm, D_MODEL), F32)],
        compiler_params=_params(("arbitrary",)),
        name="out_proj_conv_ffn",
    )(x, a_pool, a_attn, a_mem, w_out.astype(BF16), norm2.reshape(1, D_MODEL), w_up.astype(BF16),
      conv_w, conv_b.reshape(1, 2 * D_FF), w_down.astype(BF16), hist)


def _pad_rows(a, rows, axis):
    pad = [(0, 0)] * a.ndim
    pad[axis] = (0, rows - a.shape[axis])
    return jnp.pad(a, pad)


def _layer(x, pos, pos0, keys_past, mk, mv, pool_hist, conv_hist, lw, *, causal, tm_proj, tm_pool,
           tm_ffn):
    b, t, _ = x.shape
    xf = x.reshape(b * t, D_MODEL)
    (u, qx, k, kb, v, vt, ki, kir, qi, wi, qm) = _project(
        xf, pos, lw['norm1'], lw['wp'], lw['q_norm'], lw['k_norm'], lw['mem_q_norm'], lw['bmat'],
        tm_proj)

    hist16 = jnp.pad(pool_hist, ((0, 0), (2 * SUBLANES - POOL_HIST, 0), (0, 0)))
    a_pool, a_mem = _pool_mem(u, hist16, lw['pw_bd'], lw['pool_scale'], qm, mk, mv, tm_pool,
                              t // tm_pool, pos0)

    if keys_past is None:
        l_keys = t
        top_k = min(TOPK_MAX, l_keys // 4)
        lp = -(-l_keys // KTILE) * KTILE
        kb_all = _pad_rows(kb.reshape(b, t, LANES), lp, 1)
        vt_all = _pad_rows(vt.reshape(VT_ROWS, b, t).transpose(1, 0, 2), lp, 2)
        kir_all = _pad_rows(kir.reshape(b, t, -1), lp, 1)
        limits = ((jnp.arange(t, dtype=I32) // CHUNK + 1) * CHUNK)
        limits = jnp.tile(limits.reshape(1, t // QBLK, 1, QBLK), (b, 1, 1, 1)).reshape(-1, 1, QBLK)
        qx_b = qx.reshape(N_HEADS, b, t, LANES).transpose(1, 0, 2, 3)
        qi_b = qi.reshape(b, t, -1)
        wi_b = wi.reshape(b, t, LANES)
        a_attn = _dsa(limits, qx_b, qi_b, wi_b, kb_all, vt_all, kir_all, causal=True, top_k=top_k)
        a_attn = a_attn.reshape(b * t, N_HEADS * HEAD_DIM)
    else:
        ck, cv, cki = keys_past
        p_len = ck.shape[1]
        l_keys = p_len + t
        top_k = min(TOPK_MAX, l_keys // 4)
        lp = -(-l_keys // KTILE) * KTILE
        kb_all = _pad_rows(jnp.concatenate([ck.astype(BF16), kb.reshape(b, t, LANES)], axis=1), lp, 1)
        v_new_t = vt.reshape(VT_ROWS, b, t).transpose(1, 0, 2)
        ones_rows = jnp.zeros((b, VT_ROWS - LANES, p_len), BF16).at[:, 0, :].set(1.0)
        cv_t = jnp.concatenate([cv.astype(BF16).transpose(0, 2, 1), ones_rows], axis=1)
        vt_all = _pad_rows(jnp.concatenate([cv_t, v_new_t], axis=2), lp, 2)
        kir_all = _pad_rows(jnp.concatenate(
            [jnp.tile(cki.astype(BF16), (1, 1, IDX_HEADS)), kir.reshape(b, t, -1)], axis=1), lp, 1)
        reps = QBLK // t
        limits = jnp.full((b, 1, QBLK), l_keys, I32)
        qx_b = jnp.tile(qx.reshape(N_HEADS, b, t, LANES).transpose(1, 0, 2, 3), (1, 1, reps, 1))
        qi_b = jnp.tile(qi.reshape(b, t, -1), (1, reps, 1))
        wi_b = jnp.tile(wi.reshape(b, t, LANES), (1, reps, 1))
        a_attn = _dsa(limits, qx_b, qi_b, wi_b, kb_all, vt_all, kir_all, causal=False, top_k=top_k)
        a_attn = a_attn[:, :t].reshape(b * t, N_HEADS * HEAD_DIM)

    hist8 = jnp.pad(conv_hist, ((0, 0), (SUBLANES - (CONV_W - 1), 0), (0, 0)))
    y, conv_state = _out_ffn(xf, a_pool, a_attn, a_mem, lw['w_out'], lw['norm2'], lw['w_up'],
                             lw['conv_w'], lw['conv_b'], lw['w_down'], hist8, tm_ffn, t // tm_ffn)
    y = y.reshape(b, t, D_MODEL)
    k4 = k.reshape(b, t, N_KV_HEADS, HEAD_DIM)
    v4 = v.reshape(b, t, N_KV_HEADS, HEAD_DIM)
    ki3 = ki.reshape(b, t, IDX_DIM)
    u3 = u.reshape(b, t, POOL_WIDTH)
    pool_state = jnp.concatenate([pool_hist, u3], axis=1)[:, -POOL_HIST:]
    return y, k4, v4, ki3, pool_state, conv_state


def kernel(x_prompt, x_sample, mem_prompt, cache_k, cache_v, cache_kidx, cache_mem_k, cache_mem_v,
           state_pool, state_ffn_conv, norm1, w_in, q_norm, k_norm, pool_w, pool_scale, mem_norm,
           w_mem_k, w_mem_v, mem_q_norm, mem_k_norm, w_out, norm2, w_up, conv_w, conv_b, w_down):
    depth = norm1.shape[0]
    bp, sp, _ = x_prompt.shape
    bs, ts, _ = x_sample.shape
    p_len = cache_k.shape[2]
    blk = np.kron(np.eye(LANES // HEAD_DIM), np.ones((HEAD_DIM, HEAD_DIM))) / HEAD_DIM
    bmat = jnp.asarray(blk, BF16)
    xp, xs = x_prompt, x_sample
    p_states, s_states = [], []
    for l in range(depth):
        gw = POOL_WIDTH // POOL_GROUPS
        pw_bd = jnp.zeros((POOL_WIDTH, POOL_WIDTH), F32)
        for g in range(POOL_GROUPS):
            pw_bd = pw_bd.at[g * gw:(g + 1) * gw, g * gw:(g + 1) * gw].set(pool_w[l, g])
        lw = dict(norm1=norm1[l], wp=_prep_w_in(w_in[l]), q_norm=q_norm[l], k_norm=k_norm[l],
                  mem_q_norm=mem_q_norm[l], bmat=bmat, pw_bd=pw_bd.astype(BF16),
                  pool_scale=pool_scale[l], w_out=w_out[l], norm2=norm2[l], w_up=w_up[l],
                  conv_w=conv_w[l], conv_b=conv_b[l], w_down=w_down[l])

        mks, mvs = [], []
        for b in range(bp):
            mk_b, mv_b = _memory_kv(mem_prompt[b], mem_norm[l], w_mem_k[l], w_mem_v[l],
                                    mem_k_norm[l], bmat)
            mks.append(mk_b)
            mvs.append(mv_b)
        mk_p, mv_p = jnp.stack(mks), jnp.stack(mvs)

        xp, k_p, v_p, ki_p, pool_p, conv_p = _layer(
            xp, jnp.arange(sp), 0, None, mk_p, mv_p,
            jnp.zeros((bp, POOL_HIST, POOL_WIDTH), F32), jnp.zeros((bp, CONV_W - 1, 2 * D_FF), F32),
            lw, causal=True, tm_proj=512, tm_pool=512, tm_ffn=1024)
        p_states.append((k_p, v_p, ki_p, mk_p.reshape(bp, MEM_TOKENS, MEM_HEADS, HEAD_DIM),
                         mv_p.reshape(bp, MEM_TOKENS, MEM_HEADS, HEAD_DIM), pool_p, conv_p))

        pos_s = jnp.tile(p_len + jnp.arange(ts), bs)
        xs, k_s, v_s, ki_s, pool_s, conv_s = _layer(
            xs, pos_s, p_len,
            (cache_k[l].reshape(bs, p_len, LANES), cache_v[l].reshape(bs, p_len, LANES), cache_kidx[l]),
            cache_mem_k[l].reshape(bs, MEM_TOKENS, MEM_WIDTH),
            cache_mem_v[l].reshape(bs, MEM_TOKENS, MEM_WIDTH),
            state_pool[l], state_ffn_conv[l], lw, causal=False, tm_proj=ts * bs, tm_pool=ts, tm_ffn=ts)
        s_states.append((k_s, v_s, ki_s, pool_s, conv_s))

    k_p, v_p, kidx_p, memk_p, memv_p, pool_p, conv_p = [jnp.stack(z) for z in zip(*p_states)]
    k_s, v_s, kidx_s, pool_s, conv_s = [jnp.stack(z) for z in zip(*s_states)]
    return (xp, xs, k_p, v_p, kidx_p, memk_p, memv_p, pool_p, conv_p, k_s, v_s, kidx_s, pool_s, conv_s)
```

```python
import functools

import numpy as np
import jax
import jax.numpy as jnp
from jax import lax
from jax.experimental import pallas as pl
from jax.experimental.pallas import tpu as pltpu

F32, BF16, I32 = jnp.float32, jnp.bfloat16, jnp.int32

D_MODEL = 1024
CHUNK = 64
HEAD_DIM = 64
N_HEADS = 8
N_KV_HEADS = 2
IDX_HEADS = 8
IDX_DIM = 32
TOPK_MAX = 256
MEM_TOKENS = 256
MEM_HEADS = 4
MEM_WIDTH = MEM_HEADS * HEAD_DIM
POOL_WIDTH = 256
POOL_GROUPS = 4
POOL_WINDOWS = (2, 4, 8, 16)
POOL_HIST = 15
D_FF = 2816
CONV_W = 3
ROPE_THETA = 500000.0
ROT_DIM = HEAD_DIM // 4
IDX_ROT_DIM = IDX_DIM // 4
EPS = 1e-6
SPLITS = (POOL_WIDTH, N_HEADS * HEAD_DIM, N_KV_HEADS * HEAD_DIM, N_KV_HEADS * HEAD_DIM,
          IDX_HEADS * IDX_DIM, IDX_DIM, IDX_HEADS, MEM_WIDTH)

LANES = 128
SUBLANES = 8
VMEM_LIMIT = 56 * 1024 * 1024

COL_U = 0
COL_Q = 256
COL_K = 768
COL_V = 896
COL_QI = 1024
COL_KI = 1280
COL_WI = 1536
COL_QM = 1664
PROJ_W = 1920
ROPE_COLS = 32

ROW_TILE_PROJ = 512
ROW_TILE_POOL = 512
ROW_TILE_FFN = 1024

QBLK = 128
KTILE = 512
KUNROLL = 4
COUNT_CHAINS = 8
TIE_FIX_GROUPS = 8
LOG2E = 1.4426950408889634
Q_SCALE = HEAD_DIM ** -0.5 * LOG2E
FFN_CHUNK = 256
ALL_VISIBLE = -3.0e38
TINY = 1.1754944e-38
SCALE_JUMP = 2.0 ** -6
MAX_SEARCH_STEPS = 96
MIN_SEARCH_STEPS = 16
VT_ROWS = 144
FAST_SHIFT_LIMIT = 60.0


def _dot(a, b):
    return jnp.dot(a, b, preferred_element_type=F32)


def _dot_nt(a, b):
    return lax.dot_general(a, b, (((1,), (1,)), ((), ())), preferred_element_type=F32)


def _full(shape):
    n = len(shape)
    return pl.BlockSpec(shape, lambda *_: (0,) * n)


def _resident(shape):
    n = len(shape)
    return pl.BlockSpec(shape, lambda *_: (0,) * n, pipeline_mode=pl.Buffered(1))


def _params(sem):
    return pltpu.CompilerParams(dimension_semantics=sem, vmem_limit_bytes=VMEM_LIMIT)


def _rms(x, g):
    ms = jnp.mean(x * x, axis=-1, keepdims=True)
    return x * lax.rsqrt(ms + EPS) * g


def _head_mean_sq(x, bmat):
    sq = x * x
    hi = sq.astype(BF16)
    lo = (sq - hi.astype(F32)).astype(BF16)
    return _dot(hi, bmat) + _dot(lo, bmat)


def _memkv_body(mem_ref, g_ref, wk_ref, wv_ref, gk_ref, bmat_ref, mk_ref, mv_ref):
    m = _rms(mem_ref[...], g_ref[...]).astype(BF16)
    kk = _dot(m, wk_ref[...])
    bmat = bmat_ref[...]
    for c in range(MEM_WIDTH // LANES):
        kc = kk[:, c * LANES:(c + 1) * LANES]
        ms = _head_mean_sq(kc, bmat)
        mk_ref[:, c * LANES:(c + 1) * LANES] = kc * lax.rsqrt(ms + EPS) * gk_ref[...]
    mv_ref[...] = _dot(m, wv_ref[...])


def _memory_kv(mem, mem_norm, w_mem_k, w_mem_v, mem_k_norm, bmat):
    m = mem.shape[0]
    return pl.pallas_call(
        _memkv_body,
        grid=(1,),
        in_specs=[_full((m, D_MODEL)), _full((1, D_MODEL)), _full((D_MODEL, MEM_WIDTH)),
                  _full((D_MODEL, MEM_WIDTH)), _full((1, LANES)), _full((LANES, LANES))],
        out_specs=[_full((m, MEM_WIDTH)), _full((m, MEM_WIDTH))],
        out_shape=[jax.ShapeDtypeStruct((m, MEM_WIDTH), F32)] * 2,
        compiler_params=_params(("arbitrary",)),
        name="memory_kv",
    )(mem, mem_norm.reshape(1, D_MODEL), w_mem_k.astype(BF16), w_mem_v.astype(BF16),
      jnp.tile(mem_k_norm, 2).reshape(1, LANES), bmat)


def _rope(x, c, sa, sb, shift):
    return (x * c + pltpu.roll(x, LANES - shift, axis=1) * sa + pltpu.roll(x, shift, axis=1) * sb)


def _proj_body(x_ref, tab_ref, e_ref, g1_ref, w_ref,
               gq_ref, gk_ref, gm_ref, bmat_ref,
               u_ref, qx_ref, k_ref, kb_ref, v_ref, vt_ref, ki_ref, kir_ref, qi_ref, wi_ref, qm_ref):
    nb = _rms(x_ref[...], g1_ref[...]).astype(BF16)
    proj = _dot(nb, w_ref[...])
    u_ref[...] = proj[:, COL_U:COL_U + POOL_WIDTH]

    bmat = bmat_ref[...]
    tabs = _expand_rope(tab_ref[...], e_ref[...])
    cq, saq, sbq, ci, sai, sbi = [tabs[:, i * LANES:(i + 1) * LANES] for i in range(6)]
    half_q = ROT_DIM // 2

    def head_norm(xc, g):
        ms = _head_mean_sq(xc, bmat)
        return xc * lax.rsqrt(ms + EPS) * g

    lane = lax.broadcasted_iota(I32, (1, LANES), 1)
    low = lane < HEAD_DIM
    n_pair = N_HEADS // 2
    for g in range(n_pair):
        qc = proj[:, COL_Q + g * LANES:COL_Q + (g + 1) * LANES]
        qc = _rope(head_norm(qc, gq_ref[...]), cq, saq, sbq, half_q) * Q_SCALE
        qx_ref[g] = jnp.where(low, qc, 0.0).astype(BF16)
        qx_ref[n_pair + g] = jnp.where(low, 0.0, qc).astype(BF16)

    kc = _rope(head_norm(proj[:, COL_K:COL_K + LANES], gk_ref[...]), cq, saq, sbq, half_q)
    k_ref[...] = kc
    kb_ref[...] = kc.astype(BF16)

    vc = proj[:, COL_V:COL_V + LANES]
    v_ref[...] = vc
    vt_ref[0:LANES, :] = vc.T.astype(BF16)
    extra = lax.broadcasted_iota(I32, (VT_ROWS - LANES, vc.shape[0]), 0)
    vt_ref[LANES:VT_ROWS, :] = jnp.where(extra == 0, 1.0, 0.0).astype(BF16)

    half_i = IDX_ROT_DIM // 2
    for c in range(IDX_HEADS * IDX_DIM // LANES):
        sl = slice(c * LANES, (c + 1) * LANES)
        qic = _rope(proj[:, COL_QI + c * LANES:COL_QI + (c + 1) * LANES], ci, sai, sbi, half_i)
        qi_ref[:, sl] = (qic * (IDX_DIM ** -0.5)).astype(BF16)
        kic = _rope(proj[:, COL_KI + c * LANES:COL_KI + (c + 1) * LANES], ci, sai, sbi, half_i)
        kir_ref[:, sl] = kic.astype(BF16)
        if c == 0:
            ki_ref[...] = kic[:, :IDX_DIM]

    wi_ref[...] = proj[:, COL_WI:COL_WI + LANES] * (IDX_HEADS ** -0.5)

    for c in range(MEM_WIDTH // LANES):
        qmc = head_norm(proj[:, COL_QM + c * LANES:COL_QM + (c + 1) * LANES], gm_ref[...])
        qm_ref[:, c * LANES:(c + 1) * LANES] = (qmc * (HEAD_DIM ** -0.5)).astype(BF16)


def _rope_angles(pos):
    cols = []
    for rot_dim in (ROT_DIM, IDX_ROT_DIM):
        half = rot_dim // 2
        inv = ROPE_THETA ** (-jnp.arange(half, dtype=F32) / half)
        ang = pos.astype(F32)[:, None] * inv[None, :]
        cols += [jnp.cos(ang), jnp.sin(ang)]
    used = ROT_DIM + IDX_ROT_DIM
    cols += [jnp.ones((pos.shape[0], 1), F32), jnp.zeros((pos.shape[0], ROPE_COLS - used - 1), F32)]
    return jnp.concatenate(cols, axis=1)


def _rope_expansion():
    e = np.zeros((ROPE_COLS, 6 * LANES), np.float32)
    ones_row = ROT_DIM + IDX_ROT_DIM
    base = 0
    for fam, (rot_dim, period) in enumerate(((ROT_DIM, HEAD_DIM), (IDX_ROT_DIM, IDX_DIM))):
        half = rot_dim // 2
        for l in range(LANES):
            r = l % period
            c_col, sa_col, sb_col = (3 * fam) * LANES + l, (3 * fam + 1) * LANES + l, (3 * fam + 2) * LANES + l
            if r < half:
                e[base + r, c_col] = 1.0
                e[base + half + r, sa_col] = -1.0
            elif r < rot_dim:
                e[base + r - half, c_col] = 1.0
                e[base + half + r - half, sb_col] = 1.0
            else:
                e[ones_row, c_col] = 1.0
        base += rot_dim
    return jnp.asarray(e, BF16)


def _expand_rope(tab, e):
    t1 = tab.astype(BF16)
    r1 = tab - t1.astype(F32)
    t2 = r1.astype(BF16)
    t3 = (r1 - t2.astype(F32)).astype(BF16)
    return _dot(t1, e) + _dot(t2, e) + _dot(t3, e)


def _prep_w_in(w_in):
    cuts = np.cumsum(SPLITS)[:-1]
    wu, wq, wk, wv, wqi, wki, wwi, wqm = jnp.split(w_in, [int(c) for c in cuts], axis=1)
    order = [h for g in range(N_HEADS // 2) for h in (g, N_HEADS // 2 + g)]
    wq = wq.reshape(D_MODEL, N_HEADS, HEAD_DIM)[:, order, :].reshape(D_MODEL, N_HEADS * HEAD_DIM)
    wki = jnp.tile(wki, (1, IDX_HEADS))
    wwi = jnp.pad(wwi, ((0, 0), (0, LANES - IDX_HEADS)))
    return jnp.concatenate([wu, wq, wk, wv, wqi, wki, wwi, wqm], axis=1).astype(BF16)


def _project(x, pos, norm1, wp, q_norm, k_norm, mem_q_norm, bmat, tm):
    t = x.shape[0]
    row = lambda w: pl.BlockSpec((tm, w), lambda i: (i, 0))
    out_shape = [
        jax.ShapeDtypeStruct((t, POOL_WIDTH), F32),
        jax.ShapeDtypeStruct((N_HEADS, t, LANES), BF16),
        jax.ShapeDtypeStruct((t, LANES), F32),
        jax.ShapeDtypeStruct((t, LANES), BF16),
        jax.ShapeDtypeStruct((t, LANES), F32),
        jax.ShapeDtypeStruct((VT_ROWS, t), BF16),
        jax.ShapeDtypeStruct((t, IDX_DIM), F32),
        jax.ShapeDtypeStruct((t, IDX_HEADS * IDX_DIM), BF16),
        jax.ShapeDtypeStruct((t, IDX_HEADS * IDX_DIM), BF16),
        jax.ShapeDtypeStruct((t, LANES), F32),
        jax.ShapeDtypeStruct((t, MEM_WIDTH), BF16),
    ]
    out_specs = [
        row(POOL_WIDTH),
        pl.BlockSpec((N_HEADS, tm, LANES), lambda i: (0, i, 0)),
        row(LANES), row(LANES), row(LANES),
        pl.BlockSpec((VT_ROWS, tm), lambda i: (0, i)),
        row(IDX_DIM), row(IDX_HEADS * IDX_DIM), row(IDX_HEADS * IDX_DIM), row(LANES), row(MEM_WIDTH),
    ]
    return pl.pallas_call(
        _proj_body,
        grid=(t // tm,),
        in_specs=[row(D_MODEL), row(ROPE_COLS), _full((ROPE_COLS, 6 * LANES)), _full((1, D_MODEL)),
                  _resident((D_MODEL, PROJ_W)), _full((1, LANES)), _full((1, LANES)),
                  _full((1, LANES)), _full((LANES, LANES))],
        out_specs=out_specs,
        out_shape=out_shape,
        compiler_params=_params(("arbitrary",)),
        name="input_projection",
    )(x, _rope_angles(pos), _rope_expansion(), norm1.reshape(1, D_MODEL), wp,
      jnp.tile(q_norm, 2).reshape(1, LANES), jnp.tile(k_norm, 2).reshape(1, LANES),
      jnp.tile(mem_q_norm, 2).reshape(1, LANES), bmat)


def _pool_mem_body(u_ref, hist_ref, pw_ref, ps_ref, qm_ref, mk_ref, mv_ref,
                   apool_ref, amem_ref, carry_ref, *, tiles_per_seq, pos0):
    j = pl.program_id(0)
    tm = u_ref.shape[0]
    tile_in_seq = j % tiles_per_seq

    @pl.when(tile_in_seq == 0)
    def _():
        carry_ref[...] = hist_ref[0]

    u = u_ref[...]
    ext = jnp.concatenate([carry_ref[...], u], axis=0)
    carry_ref[...] = u[tm - 2 * SUBLANES:, :]
    s2 = ext + pltpu.roll(ext, 1, axis=0)
    s4 = s2 + pltpu.roll(s2, 2, axis=0)
    s8 = s4 + pltpu.roll(s4, 4, axis=0)
    s16 = s8 + pltpu.roll(s8, 8, axis=0)
    hs = 2 * SUBLANES
    gw = POOL_WIDTH // POOL_GROUPS
    lane = lax.broadcasted_iota(I32, (1, POOL_WIDTH), 1)
    win_sum = jnp.where(lane < gw, s2[hs:], jnp.where(lane < 2 * gw, s4[hs:],
                        jnp.where(lane < 3 * gw, s8[hs:], s16[hs:])))
    win = jnp.where(lane < gw, POOL_WINDOWS[0], jnp.where(lane < 2 * gw, POOL_WINDOWS[1],
                    jnp.where(lane < 3 * gw, POOL_WINDOWS[2], POOL_WINDOWS[3])))
    pos = pos0 + tile_in_seq * tm + lax.broadcasted_iota(I32, (tm, 1), 0)
    cnt = jnp.minimum(win, pos + 1).astype(F32)
    z = win_sum / cnt - u
    y = _dot(z.astype(BF16), pw_ref[...]) * ps_ref[...]
    apool_ref[...] = y.astype(BF16)

    qm = qm_ref[...]
    lane_m = lax.broadcasted_iota(I32, (1, MEM_WIDTH), 1)
    mk = mk_ref[0].astype(BF16)
    mv = mv_ref[0].astype(BF16)
    zero = jnp.zeros_like(qm)
    qs = jnp.concatenate(
        [jnp.where((lane_m >= h * HEAD_DIM) & (lane_m < (h + 1) * HEAD_DIM), qm, zero)
         for h in range(MEM_HEADS)], axis=0)
    logits = _dot_nt(qs, mk)
    mx = jnp.max(logits, axis=-1, keepdims=True)
    p = jnp.exp(logits - mx)
    den = jnp.sum(p, axis=-1, keepdims=True)
    o = _dot(p.astype(BF16), mv) / den
    out = jnp.zeros((tm, MEM_WIDTH), F32)
    for h in range(MEM_HEADS):
        oh = o[h * tm:(h + 1) * tm]
        out = jnp.where((lane_m >= h * HEAD_DIM) & (lane_m < (h + 1) * HEAD_DIM), oh, out)
    amem_ref[...] = out.astype(BF16)


def _pool_mem(u, hist, pw_bd, pool_scale, qm, mk, mv, tm, tiles_per_seq, pos0):
    t = u.shape[0]
    seq = lambda j: j // tiles_per_seq
    return pl.pallas_call(
        functools.partial(_pool_mem_body, tiles_per_seq=tiles_per_seq, pos0=pos0),
        grid=(t // tm,),
        in_specs=[pl.BlockSpec((tm, POOL_WIDTH), lambda j: (j, 0)),
                  pl.BlockSpec((1, 2 * SUBLANES, POOL_WIDTH), lambda j: (seq(j), 0, 0)),
                  _full((POOL_WIDTH, POOL_WIDTH)), _full((1, POOL_WIDTH)),
                  pl.BlockSpec((tm, MEM_WIDTH), lambda j: (j, 0)),
                  pl.BlockSpec((1, MEM_TOKENS, MEM_WIDTH), lambda j: (seq(j) % mk.shape[0], 0, 0)),
                  pl.BlockSpec((1, MEM_TOKENS, MEM_WIDTH), lambda j: (seq(j) % mk.shape[0], 0, 0))],
        out_specs=[pl.BlockSpec((tm, POOL_WIDTH), lambda j: (j, 0)),
                   pl.BlockSpec((tm, MEM_WIDTH), lambda j: (j, 0))],
        out_shape=[jax.ShapeDtypeStruct((t, POOL_WIDTH), BF16),
                   jax.ShapeDtypeStruct((t, MEM_WIDTH), BF16)],
        scratch_shapes=[pltpu.VMEM((2 * SUBLANES, POOL_WIDTH), F32)],
        compiler_params=_params(("arbitrary",)),
        name="pool_and_memory_attention",
    )(u, hist, pw_bd, pool_scale.reshape(1, POOL_WIDTH), qm, mk, mv)


def _midpoint(lo, hi):
    am = 0.5 * lo + 0.5 * hi
    alo, ahi = jnp.abs(lo), jnp.abs(hi)
    small, large = jnp.minimum(alo, ahi), jnp.maximum(alo, ahi)
    sign = jnp.where(hi > 0.0, 1.0, -1.0)
    same = jnp.where(small < large * SCALE_JUMP, sign * large * SCALE_JUMP,
                     jnp.where(large > 4.0 * small, sign * (jnp.sqrt(small) * jnp.sqrt(large)), am))
    straddle = jnp.where(hi >= -lo, hi * SCALE_JUMP, 0.0)
    at_lo0 = jnp.where(hi > TINY, TINY, 0.0)
    at_hi0 = jnp.where(lo < -TINY, -TINY, 0.0)
    return jnp.where((lo < 0.0) & (hi > 0.0), straddle,
                     jnp.where(lo == 0.0, at_lo0, jnp.where(hi == 0.0, at_hi0, same)))


def _sublane_total(x):
    x = x + pltpu.roll(x, 4, axis=0)
    x = x + pltpu.roll(x, 2, axis=0)
    return x + pltpu.roll(x, 1, axis=0)


def _dsa_body(lim_ref, qx_ref, qi_ref, wi_ref, kb_ref, vt_ref, kir_ref, out_ref,
              score_ref, acc_ref, kmax_ref, *, causal, n_tiles_static, n_clear_static, blocks_per_seq,
              top_k):
    qb = pl.program_id(0) % blocks_per_seq
    tk = KTILE
    if causal:
        n_tiles = ((qb + 1) * QBLK + tk - 1) // tk
        n_clear = (qb * QBLK + CHUNK) // tk
    else:
        n_tiles = n_tiles_static
        n_clear = n_clear_static
    limit = lim_ref[0]

    @pl.when(qb == 0)
    def _():
        ones = jnp.ones((LANES, LANES), BF16)

        def knorm(i, mx):
            kt = kb_ref[pl.ds(pl.multiple_of(i * tk, tk), tk), :].astype(F32)
            return jnp.maximum(mx, jnp.max(_dot((kt * kt).astype(BF16), ones), axis=0, keepdims=True))

        kmax_ref[...] = lax.fori_loop(0, n_tiles_static, knorm, jnp.zeros((1, LANES), F32))

    qi = qi_ref[...]
    lane_i = lax.broadcasted_iota(I32, (1, IDX_HEADS * IDX_DIM), 1)
    zero_qi = jnp.zeros_like(qi)
    qi_rows = jnp.concatenate(
        [jnp.where((lane_i >= h * IDX_DIM) & (lane_i < (h + 1) * IDX_DIM), qi, zero_qi)
         for h in range(IDX_HEADS)], axis=0)
    w_t = wi_ref[...].T
    n_cls = TOPK_MAX

    def score_tile(r0, cls_max, hide):
        s = _dot_nt(kir_ref[pl.ds(r0, tk), :], qi_rows)
        score = jnp.zeros((tk, QBLK), F32)
        for h in range(IDX_HEADS):
            score = score + jnp.maximum(s[:, h * QBLK:(h + 1) * QBLK], 0.0) * w_t[h:h + 1, :]
        if hide:
            kidx = r0 + lax.broadcasted_iota(I32, (tk, 1), 0)
            score = jnp.where(kidx < limit, score, -jnp.inf)
        score_ref[pl.ds(r0, tk), :] = score
        for c in range(tk // n_cls):
            cls_max = jnp.maximum(cls_max, score[c * n_cls:(c + 1) * n_cls])
        return cls_max

    def score_step(i, cls_max, hide):
        for sub in range(KUNROLL):
            cls_max = score_tile(pl.multiple_of((i * KUNROLL + sub) * tk, tk), cls_max, hide)
        return cls_max

    n_steps = n_tiles // KUNROLL
    tail0 = n_steps * KUNROLL
    clear_steps = n_clear // KUNROLL
    cls_max = lax.fori_loop(0, clear_steps, lambda i, c: score_step(i, c, False),
                            jnp.full((n_cls, QBLK), -jnp.inf, F32))
    cls_max = lax.fori_loop(clear_steps, n_steps, lambda i, c: score_step(i, c, True), cls_max)
    cls_max = lax.fori_loop(tail0, n_tiles,
                            lambda i, c: score_tile(pl.multiple_of(i * tk, tk), c, True), cls_max)
    lo0 = jnp.min(cls_max, axis=0, keepdims=True)
    top = jnp.max(cls_max, axis=0, keepdims=True)
    hi0 = top + (jnp.abs(top) * 2.0 ** -20 + 1e-30)
    few = limit <= top_k
    rep = lambda x: jnp.broadcast_to(x, (SUBLANES, QBLK))
    lo0 = rep(jnp.where(few, ALL_VISIBLE, jnp.maximum(lo0, ALL_VISIBLE)))
    hi0 = rep(hi0)

    def count_ge(thr):
        def count_rows(r0, rows, acc):
            sc = score_ref[pl.ds(r0, rows), :]
            chains = [None] * COUNT_CHAINS
            for g in range(rows // SUBLANES):
                ind = (sc[g * SUBLANES:(g + 1) * SUBLANES] >= thr).astype(I32)
                c = g % COUNT_CHAINS
                chains[c] = ind if chains[c] is None else chains[c] + ind
            while len(chains) > 1:
                chains = [a + b for a, b in zip(chains[0::2], chains[1::2])]
            return acc + chains[0]

        rows = KUNROLL * tk
        acc = lax.fori_loop(0, n_steps,
                            lambda i, a: count_rows(pl.multiple_of(i * rows, rows), rows, a),
                            jnp.zeros((SUBLANES, QBLK), I32))
        acc = lax.fori_loop(tail0, n_tiles,
                            lambda i, a: count_rows(pl.multiple_of(i * tk, tk), tk, a), acc)
        return _sublane_total(acc)

    def probe(lo, hi, c_lo):
        mid = _midpoint(lo, hi)
        return mid, (c_lo != top_k) & (mid > lo) & (mid < hi)

    def search_step(st):
        lo, hi, c_lo, c_hi = st
        mid, open_ = probe(lo, hi, c_lo)
        c = count_ge(mid)
        up = open_ & (c >= top_k)
        dn = open_ & (c < top_k)
        return (jnp.where(up, mid, lo), jnp.where(dn, mid, hi),
                jnp.where(up, c, c_lo), jnp.where(dn, c, c_hi))

    def search_cond(st):
        step, lo, hi, c_lo, _ = st
        any_open = jnp.max(probe(lo, hi, c_lo)[1][0:1, :].astype(F32)) > 0.0
        return (step < MAX_SEARCH_STEPS) & any_open

    big = jnp.full((SUBLANES, QBLK), 2 * top_k, I32)
    c_lo0 = jnp.where(rep(few), top_k, big)
    st = lax.fori_loop(0, MIN_SEARCH_STEPS, lambda _, st: search_step(st),
                       (lo0, hi0, c_lo0, jnp.zeros_like(big)))
    _, thr, _, c_lo, c_hi = lax.while_loop(
        search_cond, lambda st: (st[0] + 1,) + search_step(st[1:]), (jnp.int32(MIN_SEARCH_STEPS),) + st)

    tied = c_lo != top_k
    need = jnp.where(tied, top_k - c_hi, jnp.int32(2 ** 30))

    @pl.when(jnp.max(tied.astype(I32)) > 0)
    def _():
        sub = lax.broadcasted_iota(I32, (SUBLANES, QBLK), 0)

        def fix_rows(r0, seen):
            sc = score_ref[pl.ds(r0, SUBLANES), :]
            eq = (sc == thr).astype(I32)
            pre = eq
            for sft in (1, 2, 4):
                pre = pre + jnp.where(sub >= sft, pltpu.roll(pre, sft, axis=0), 0)
            rank = seen + pre - eq
            score_ref[pl.ds(r0, SUBLANES), :] = jnp.where((eq > 0) & (rank >= need), -jnp.inf, sc)
            return seen + _sublane_total(eq)

        def fix(v, seen):
            for g in range(TIE_FIX_GROUPS):
                seen = fix_rows(pl.multiple_of((v * TIE_FIX_GROUPS + g) * SUBLANES, SUBLANES), seen)
            return seen

        lax.fori_loop(0, n_tiles * (tk // (SUBLANES * TIE_FIX_GROUPS)), fix,
                      jnp.zeros((SUBLANES, QBLK), I32))

    qx = qx_ref[...].reshape(N_HEADS * QBLK, LANES)
    thr_row = thr[0:1, :]
    acc_ref[...] = jnp.zeros_like(acc_ref)

    qf = qx.astype(F32)
    qn2 = _dot_nt(jnp.ones((SUBLANES, LANES), BF16), (qf * qf).astype(BF16))[0:1, :]
    kmax2 = jnp.concatenate([kmax_ref[...]] * N_HEADS, axis=1)
    bound = jnp.sqrt(qn2 * kmax2) * 1.01
    fast = jnp.max(bound) <= FAST_SHIFT_LIMIT

    def finish(o):
        hpk = N_HEADS // N_KV_HEADS
        for g in range(N_HEADS // 2):
            parts = []
            for h in (2 * g, 2 * g + 1):
                jkv = h // hpk
                parts.append(o[jkv * HEAD_DIM:(jkv + 1) * HEAD_DIM, h * QBLK:(h + 1) * QBLK])
            blk = jnp.concatenate(parts, axis=0)
            out_ref[:, g * LANES:(g + 1) * LANES] = blk.T.astype(BF16)

    @pl.when(fast)
    def _():
        def attend_tile(r0):
            logit = _dot_nt(kb_ref[pl.ds(r0, tk), :], qx)
            sel = score_ref[pl.ds(r0, tk), :] >= thr_row
            ps = []
            for h in range(N_HEADS):
                sl = slice(h * QBLK, (h + 1) * QBLK)
                ps.append(jnp.where(sel, jnp.exp2(logit[:, sl] - bound[:, sl]), 0.0).astype(BF16))
            return _dot(vt_ref[:, pl.ds(r0, tk)], jnp.concatenate(ps, axis=1))

        def attend(i, carry):
            pv = attend_tile(pl.multiple_of(i * KUNROLL * tk, tk))
            for sub in range(1, KUNROLL):
                pv = pv + attend_tile(pl.multiple_of((i * KUNROLL + sub) * tk, tk))
            acc_ref[...] += pv
            return carry

        def attend_one(i, carry):
            acc_ref[...] += attend_tile(pl.multiple_of(i * tk, tk))
            return carry

        lax.fori_loop(0, n_steps, attend, 0)
        lax.fori_loop(tail0, n_tiles, attend_one, 0)
        acc = acc_ref[...]
        finish(acc[:LANES] / acc[LANES:LANES + 1])

    @pl.when(jnp.logical_not(fast))
    def _():
        def attend(i, m):
            r0 = pl.multiple_of(i * tk, tk)
            logit = _dot_nt(kb_ref[pl.ds(r0, tk), :], qx)
            sel = score_ref[pl.ds(r0, tk), :] >= thr_row
            m_new = jnp.maximum(m, jnp.max(logit, axis=0, keepdims=True))
            alpha = jnp.exp2(m - m_new)
            ps = []
            for h in range(N_HEADS):
                sl = slice(h * QBLK, (h + 1) * QBLK)
                ps.append(jnp.where(sel, jnp.exp2(logit[:, sl] - m_new[:, sl]), 0.0).astype(BF16))
            p = jnp.concatenate(ps, axis=1)
            acc_ref[...] = acc_ref[...] * alpha + _dot(vt_ref[:, pl.ds(r0, tk)], p)
            return m_new

        lax.fori_loop(0, n_tiles, attend, jnp.full((1, N_HEADS * QBLK), -1e30, F32))
        acc = acc_ref[...]
        finish(acc[:LANES] / acc[LANES:LANES + 1])


def _dsa(limits, qx, qi, wi, kb, vt, kir, *, causal, n_keys, top_k):
    nb, _, t, _ = qx.shape
    lp = kb.shape[1]
    nq = t // QBLK
    body = functools.partial(_dsa_body, causal=causal, n_tiles_static=lp // KTILE,
                             n_clear_static=n_keys // KTILE, blocks_per_seq=nq, top_k=top_k)
    bq = lambda i: (i // nq, i % nq)
    keys_mode = pl.Buffered(1) if nq > 1 else None
    return pl.pallas_call(
        body,
        grid=(nb * nq,),
        in_specs=[
            pl.BlockSpec((1, 1, QBLK), lambda i: (i, 0, 0)),
            pl.BlockSpec((None, N_HEADS, QBLK, LANES), lambda i: (bq(i)[0], 0, bq(i)[1], 0)),
            pl.BlockSpec((None, QBLK, IDX_HEADS * IDX_DIM), lambda i: (bq(i)[0], bq(i)[1], 0)),
            pl.BlockSpec((None, QBLK, LANES), lambda i: (bq(i)[0], bq(i)[1], 0)),
            pl.BlockSpec((None, lp, LANES), lambda i: (bq(i)[0], 0, 0), pipeline_mode=keys_mode),
            pl.BlockSpec((None, VT_ROWS, lp), lambda i: (bq(i)[0], 0, 0), pipeline_mode=keys_mode),
            pl.BlockSpec((None, lp, IDX_HEADS * IDX_DIM), lambda i: (bq(i)[0], 0, 0),
                         pipeline_mode=keys_mode),
        ],
        out_specs=pl.BlockSpec((None, QBLK, N_HEADS * HEAD_DIM), lambda i: (bq(i)[0], bq(i)[1], 0)),
        out_shape=jax.ShapeDtypeStruct((nb, t, N_HEADS * HEAD_DIM), BF16),
        scratch_shapes=[pltpu.VMEM((lp, QBLK), F32), pltpu.VMEM((VT_ROWS, N_HEADS * QBLK), F32),
                        pltpu.VMEM((1, LANES), F32)],
        compiler_params=_params(("arbitrary",)),
        name="dsa_attention",
    )(limits, qx, qi, wi, kb, vt, kir)


def _ffn_body(x_ref, ap_ref, aa_ref, am_ref, wo_ref, g2_ref, wup_ref, cw_ref, cb_ref, wdn_ref,
              hist_ref, y_ref, cst_ref, carry_ref, acc_ref, *, tiles_per_seq):
    j = pl.program_id(0)
    tm = x_ref.shape[0]
    tile_in_seq = j % tiles_per_seq

    @pl.when(tile_in_seq == 0)
    def _():
        carry_ref[...] = hist_ref[0]

    mix = jnp.concatenate([ap_ref[...], aa_ref[...], am_ref[...]], axis=1)
    h = x_ref[...] + _dot(mix, wo_ref[...])
    nb = _rms(h, g2_ref[...]).astype(BF16)
    acc_ref[...] = h
    n_chunk = D_FF // FFN_CHUNK
    for c in range(n_chunk):
        conv = []
        for part in range(2):
            c0 = part * D_FF + c * FFN_CHUNK
            cols = slice(c0, c0 + FFN_CHUNK)
            up = _dot(nb, wup_ref[:, cols])
            ext = jnp.concatenate([carry_ref[:, cols], up], axis=0)
            carry_ref[:, cols] = up[tm - SUBLANES:, :]
            cst_ref[0, :, cols] = up[tm - (CONV_W - 1):, :]

            cv = (cb_ref[:, cols] + ext[SUBLANES:] * cw_ref[2:3, cols]
                  + pltpu.roll(ext, 1, axis=0)[SUBLANES:] * cw_ref[1:2, cols]
                  + pltpu.roll(ext, 2, axis=0)[SUBLANES:] * cw_ref[0:1, cols])
            conv.append(cv)
        gate, val = conv
        act = gate / (1.0 + jnp.exp(-gate)) * val
        acc_ref[...] += _dot(act.astype(BF16), wdn_ref[c * FFN_CHUNK:(c + 1) * FFN_CHUNK, :])
    y_ref[...] = acc_ref[...]


def _out_ffn(x, a_pool, a_attn, a_mem, w_out, norm2, w_up, conv_w, conv_b, w_down, hist, tm,
             tiles_per_seq):
    t = x.shape[0]
    n_seq = t // (tm * tiles_per_seq)
    row = lambda w: pl.BlockSpec((tm, w), lambda j: (j, 0))
    seq = lambda j: j // tiles_per_seq
    return pl.pallas_call(
        functools.partial(_ffn_body, tiles_per_seq=tiles_per_seq),
        grid=(t // tm,),
        in_specs=[row(D_MODEL), row(POOL_WIDTH), row(N_HEADS * HEAD_DIM), row(MEM_WIDTH),
                  _resident((D_MODEL, D_MODEL)), _full((1, D_MODEL)),
                  _resident((D_MODEL, 2 * D_FF)), _full((CONV_W, 2 * D_FF)), _full((1, 2 * D_FF)),
                  _resident((D_FF, D_MODEL)),
                  pl.BlockSpec((1, SUBLANES, 2 * D_FF), lambda j: (seq(j), 0, 0))],
        out_specs=[row(D_MODEL),
                   pl.BlockSpec((1, CONV_W - 1, 2 * D_FF), lambda j: (seq(j), 0, 0))],
        out_shape=[jax.ShapeDtypeStruct((t, D_MODEL), F32),
                   jax.ShapeDtypeStruct((n_seq, CONV_W - 1, 2 * D_FF), F32)],
        scratch_shapes=[pltpu.VMEM((SUBLANES, 2 * D_FF), F32), pltpu.VMEM((tm, D_MODEL), F32)],
        compiler_params=_params(("arbitrary",)),
        name="out_proj_conv_ffn",
    )(x, a_pool, a_attn, a_mem, w_out.astype(BF16), norm2.reshape(1, D_MODEL), w_up.astype(BF16),
      conv_w, conv_b.reshape(1, 2 * D_FF), w_down.astype(BF16), hist)


def _pad_rows(a, rows, axis):
    pad = [(0, 0)] * a.ndim
    pad[axis] = (0, rows - a.shape[axis])
    return jnp.pad(a, pad)


def _layer(x, pos, pos0, keys_past, mk, mv, pool_hist, conv_hist, lw, *, causal, tm_proj, tm_pool,
           tm_ffn):
    b, t, _ = x.shape
    xf = x.reshape(b * t, D_MODEL)
    (u, qx, k, kb, v, vt, ki, kir, qi, wi, qm) = _project(
        xf, pos, lw['norm1'], lw['wp'], lw['q_norm'], lw['k_norm'], lw['mem_q_norm'], lw['bmat'],
        tm_proj)

    hist16 = jnp.pad(pool_hist, ((0, 0), (2 * SUBLANES - POOL_HIST, 0), (0, 0)))
    a_pool, a_mem = _pool_mem(u, hist16, lw['pw_bd'], lw['pool_scale'], qm, mk, mv, tm_pool,
                              t // tm_pool, pos0)

    if keys_past is None:
        l_keys = t
        top_k = min(TOPK_MAX, l_keys // 4)
        lp = -(-l_keys // KTILE) * KTILE
        kb_all = _pad_rows(kb.reshape(b, t, LANES), lp, 1)
        vt_all = _pad_rows(vt.reshape(VT_ROWS, b, t).transpose(1, 0, 2), lp, 2)
        kir_all = _pad_rows(kir.reshape(b, t, -1), lp, 1)
        limits = ((jnp.arange(t, dtype=I32) // CHUNK + 1) * CHUNK)
        limits = jnp.tile(limits.reshape(1, t // QBLK, 1, QBLK), (b, 1, 1, 1)).reshape(-1, 1, QBLK)
        qx_b = qx.reshape(N_HEADS, b, t, LANES).transpose(1, 0, 2, 3)
        qi_b = qi.reshape(b, t, -1)
        wi_b = wi.reshape(b, t, LANES)
        a_attn = _dsa(limits, qx_b, qi_b, wi_b, kb_all, vt_all, kir_all, causal=True, n_keys=l_keys,
                      top_k=top_k)
        a_attn = a_attn.reshape(b * t, N_HEADS * HEAD_DIM)
    else:
        ck, cv, cki = keys_past
        p_len = ck.shape[1]
        l_keys = p_len + t
        top_k = min(TOPK_MAX, l_keys // 4)
        lp = -(-l_keys // KTILE) * KTILE
        kb_all = _pad_rows(jnp.concatenate([ck.astype(BF16), kb.reshape(b, t, LANES)], axis=1), lp, 1)
        v_new_t = vt.reshape(VT_ROWS, b, t).transpose(1, 0, 2)
        ones_rows = jnp.zeros((b, VT_ROWS - LANES, p_len), BF16).at[:, 0, :].set(1.0)
        cv_t = jnp.concatenate([cv.astype(BF16).transpose(0, 2, 1), ones_rows], axis=1)
        vt_all = _pad_rows(jnp.concatenate([cv_t, v_new_t], axis=2), lp, 2)
        kir_all = _pad_rows(jnp.concatenate(
            [jnp.tile(cki.astype(BF16), (1, 1, IDX_HEADS)), kir.reshape(b, t, -1)], axis=1), lp, 1)
        reps = QBLK // t
        limits = jnp.full((b, 1, QBLK), l_keys, I32)
        qx_b = jnp.tile(qx.reshape(N_HEADS, b, t, LANES).transpose(1, 0, 2, 3), (1, 1, reps, 1))
        qi_b = jnp.tile(qi.reshape(b, t, -1), (1, reps, 1))
        wi_b = jnp.tile(wi.reshape(b, t, LANES), (1, reps, 1))
        a_attn = _dsa(limits, qx_b, qi_b, wi_b, kb_all, vt_all, kir_all, causal=False, n_keys=l_keys,
                      top_k=top_k)
        a_attn = a_attn[:, :t].reshape(b * t, N_HEADS * HEAD_DIM)

    hist8 = jnp.pad(conv_hist, ((0, 0), (SUBLANES - (CONV_W - 1), 0), (0, 0)))
    y, conv_state = _out_ffn(xf, a_pool, a_attn, a_mem, lw['w_out'], lw['norm2'], lw['w_up'],
                             lw['conv_w'], lw['conv_b'], lw['w_down'], hist8, tm_ffn, t // tm_ffn)
    y = y.reshape(b, t, D_MODEL)
    k4 = k.reshape(b, t, N_KV_HEADS, HEAD_DIM)
    v4 = v.reshape(b, t, N_KV_HEADS, HEAD_DIM)
    ki3 = ki.reshape(b, t, IDX_DIM)
    u3 = u.reshape(b, t, POOL_WIDTH)
    pool_state = jnp.concatenate([pool_hist, u3], axis=1)[:, -POOL_HIST:]
    return y, k4, v4, ki3, pool_state, conv_state


def kernel(x_prompt, x_sample, mem_prompt, cache_k, cache_v, cache_kidx, cache_mem_k, cache_mem_v,
           state_pool, state_ffn_conv, norm1, w_in, q_norm, k_norm, pool_w, pool_scale, mem_norm,
           w_mem_k, w_mem_v, mem_q_norm, mem_k_norm, w_out, norm2, w_up, conv_w, conv_b, w_down):
    depth = norm1.shape[0]
    bp, sp, _ = x_prompt.shape
    bs, ts, _ = x_sample.shape
    p_len = cache_k.shape[2]
    blk = np.kron(np.eye(LANES // HEAD_DIM), np.ones((HEAD_DIM, HEAD_DIM))) / HEAD_DIM
    bmat = jnp.asarray(blk, BF16)
    xp, xs = x_prompt, x_sample
    p_states, s_states = [], []
    for l in range(depth):
        gw = POOL_WIDTH // POOL_GROUPS
        pw_bd = jnp.zeros((POOL_WIDTH, POOL_WIDTH), F32)
        for g in range(POOL_GROUPS):
            pw_bd = pw_bd.at[g * gw:(g + 1) * gw, g * gw:(g + 1) * gw].set(pool_w[l, g])
        lw = dict(norm1=norm1[l], wp=_prep_w_in(w_in[l]), q_norm=q_norm[l], k_norm=k_norm[l],
                  mem_q_norm=mem_q_norm[l], bmat=bmat, pw_bd=pw_bd.astype(BF16),
                  pool_scale=pool_scale[l], w_out=w_out[l], norm2=norm2[l], w_up=w_up[l],
                  conv_w=conv_w[l], conv_b=conv_b[l], w_down=w_down[l])

        mks, mvs = [], []
        for b in range(bp):
            mk_b, mv_b = _memory_kv(mem_prompt[b], mem_norm[l], w_mem_k[l], w_mem_v[l],
                                    mem_k_norm[l], bmat)
            mks.append(mk_b)
            mvs.append(mv_b)
        mk_p, mv_p = jnp.stack(mks), jnp.stack(mvs)

        xp, k_p, v_p, ki_p, pool_p, conv_p = _layer(
            xp, jnp.arange(sp), 0, None, mk_p, mv_p,
            jnp.zeros((bp, POOL_HIST, POOL_WIDTH), F32), jnp.zeros((bp, CONV_W - 1, 2 * D_FF), F32),
            lw, causal=True, tm_proj=ROW_TILE_PROJ, tm_pool=ROW_TILE_POOL, tm_ffn=ROW_TILE_FFN)
        p_states.append((k_p, v_p, ki_p, mk_p.reshape(bp, MEM_TOKENS, MEM_HEADS, HEAD_DIM),
                         mv_p.reshape(bp, MEM_TOKENS, MEM_HEADS, HEAD_DIM), pool_p, conv_p))

        pos_s = jnp.tile(p_len + jnp.arange(ts), bs)
        xs, k_s, v_s, ki_s, pool_s, conv_s = _layer(
            xs, pos_s, p_len,
            (cache_k[l].reshape(bs, p_len, LANES), cache_v[l].reshape(bs, p_len, LANES), cache_kidx[l]),
            cache_mem_k[l].reshape(bs, MEM_TOKENS, MEM_WIDTH),
            cache_mem_v[l].reshape(bs, MEM_TOKENS, MEM_WIDTH),
            state_pool[l], state_ffn_conv[l], lw, causal=False, tm_proj=ts * bs, tm_pool=ts, tm_ffn=ts)
        s_states.append((k_s, v_s, ki_s, pool_s, conv_s))

    k_p, v_p, kidx_p, memk_p, memv_p, pool_p, conv_p = [jnp.stack(z) for z in zip(*p_states)]
    k_s, v_s, kidx_s, pool_s, conv_s = [jnp.stack(z) for z in zip(*s_states)]
    return (xp, xs, k_p, v_p, kidx_p, memk_p, memv_p, pool_p, conv_p, k_s, v_s, kidx_s, pool_s, conv_s)
```
